```python
import math
import jax, jax.numpy as jnp
from jax import lax
import numpy as np

D_MODEL = 2048
BATCH = 16
SEQ = 2048
DEPTH = 1

ROPE_THETA = 500000.0
EPS = 1e-6
Q_BLOCK = 128
MLA_HEADS = 8
MLA_Q_LORA = 512
MLA_KV_LORA = 512
MLA_NOPE = 128
MLA_ROPE = 64
MLA_V = 128
MLA_QK = MLA_NOPE + MLA_ROPE
DIFF_HEADS = 8
DIFF_D = 64
DIFF_V = 2 * DIFF_D
DIFF_ROT = DIFF_D // 4
COL_Q_LAT = MLA_Q_LORA
COL_KV_LAT = MLA_KV_LORA
COL_K_ROPE = MLA_ROPE
COL_DQ = DIFF_HEADS * 2 * DIFF_D
COL_DK = DIFF_HEADS * 2 * DIFF_D
COL_DV = DIFF_HEADS * DIFF_V
COL_GATES = 2 * D_MODEL
ATT_IN = COL_Q_LAT + COL_KV_LAT + COL_K_ROPE + COL_DQ + COL_DK + COL_DV + COL_GATES
PEER_HEADS = 8
PEER_NKEYS = 128
PEER_EXPERTS = PEER_NKEYS * PEER_NKEYS
PEER_HALF = 128
PEER_QDIM = 2 * PEER_HALF
PEER_TOPK = 16
PEER_CHUNK = 128

kernel_name = "hybrid_mla_diffattn_peer_block"


def rmsnorm(x, w):
    x32 = x.astype(jnp.float32)
    y = x32 * lax.rsqrt(jnp.mean(x32 * x32, axis=-1, keepdims=True) + EPS)
    return (y * w.astype(jnp.float32)).astype(x.dtype)


def rope(x, pos):
    r = x.shape[-1]
    inv = ROPE_THETA ** (-jnp.arange(0, r, 2, dtype=jnp.float32) / r)
    ang = pos.astype(jnp.float32)[..., None] * inv
    cos = jnp.cos(ang)[:, :, None, :]
    sin = jnp.sin(ang)[:, :, None, :]
    x32 = x.astype(jnp.float32)
    x1, x2 = x32[..., : r // 2], x32[..., r // 2 :]
    return jnp.concatenate([x1 * cos - x2 * sin, x1 * sin + x2 * cos], axis=-1).astype(x.dtype)


def blocked_causal_attention(q, k, v, scale, combine):
    B, S = q.shape[0], q.shape[1]
    nb = S // Q_BLOCK
    q_blocks = q.reshape(B, nb, Q_BLOCK, q.shape[2], q.shape[3]).swapaxes(0, 1)
    key_idx = jnp.arange(S)

    def one_block(args):
        qb, start = args
        s = jnp.einsum('bqhd,bkhd->bhqk', qb, k).astype(jnp.float32) * scale
        q_idx = start + jnp.arange(Q_BLOCK)
        s = jnp.where(key_idx[None, :] <= q_idx[:, None], s, jnp.finfo(jnp.float32).min)
        p = combine(jax.nn.softmax(s, axis=-1))
        return jnp.einsum('bhqk,bkhd->bqhd', p.astype(v.dtype), v)

    out = lax.map(one_block, (q_blocks, jnp.arange(nb) * Q_BLOCK))
    return out.swapaxes(0, 1).reshape(B, S, out.shape[3], out.shape[4])


def setup_inputs(seed: int = 0) -> dict:
    key = jax.random.key(seed)
    ks = iter(jax.random.split(key, 32))
    f32 = jnp.float32

    def nrm(shape, scale):
        return jax.random.normal(next(ks), shape, f32) * scale

    def gain(shape):
        return 1.0 + 0.02 * jax.random.normal(next(ks), shape, f32)

    L = DEPTH
    x = jax.random.normal(next(ks), (BATCH, SEQ, D_MODEL), f32)
    c = jax.random.normal(next(ks), (BATCH, D_MODEL), f32)
    offs = jax.random.randint(next(ks), (BATCH, 1), 0, 1024, dtype=jnp.int32)
    positions = jnp.arange(SEQ, dtype=jnp.int32)[None, :] + offs
    return {
        "x": x,
        "c": c,
        "positions": positions,
        "norm1_w": gain((L, D_MODEL)),
        "norm2_w": gain((L, D_MODEL)),
        "w_ada": nrm((L, D_MODEL, 6 * D_MODEL), 0.5 * D_MODEL ** -0.5),
        "b_ada": nrm((L, 6 * D_MODEL), 0.01),
        "w_att_in": nrm((L, D_MODEL, ATT_IN), D_MODEL ** -0.5),
        "mla_q_norm": gain((L, MLA_Q_LORA)),
        "w_mla_qb": nrm((L, MLA_Q_LORA, MLA_HEADS * MLA_QK), MLA_Q_LORA ** -0.5),
        "mla_kv_norm": gain((L, MLA_KV_LORA)),
        "w_mla_kvb": nrm((L, MLA_KV_LORA, MLA_HEADS * (MLA_NOPE + MLA_V)), MLA_KV_LORA ** -0.5),
        "mla_qk_norm_q": gain((L, MLA_QK)),
        "mla_qk_norm_k": gain((L, MLA_QK)),
        "w_mla_o": nrm((L, MLA_HEADS * MLA_V, D_MODEL), (MLA_HEADS * MLA_V) ** -0.5),
        "diff_q_norm": gain((L, DIFF_D)),
        "diff_k_norm": gain((L, DIFF_D)),
        "diff_lambda": nrm((L, 4, DIFF_D), 0.1),
        "diff_subln": gain((L, DIFF_V)),
        "w_diff_o": nrm((L, DIFF_HEADS * DIFF_V, D_MODEL), (DIFF_HEADS * DIFF_V) ** -0.5),
        "w_att_out": nrm((L, D_MODEL, D_MODEL), D_MODEL ** -0.5),
        "w_peer_q": nrm((L, D_MODEL, PEER_HEADS * PEER_QDIM), D_MODEL ** -0.5),
        "peer_keys": nrm((L, PEER_HEADS, 2, PEER_NKEYS, PEER_HALF), PEER_HALF ** -0.5),
        "peer_u": nrm((L, PEER_EXPERTS, D_MODEL), D_MODEL ** -0.5),
        "peer_v": nrm((L, PEER_EXPERTS, D_MODEL), 0.5),
    }


def reference(x, c, positions, norm1_w, norm2_w, w_ada, b_ada, w_att_in, mla_q_norm, w_mla_qb,
              mla_kv_norm, w_mla_kvb, mla_qk_norm_q, mla_qk_norm_k, w_mla_o, diff_q_norm,
              diff_k_norm, diff_lambda, diff_subln, w_diff_o, w_att_out, w_peer_q, peer_keys,
              peer_u, peer_v):
    B, S, D = x.shape
    for l in range(DEPTH):
        lambda_init = 0.8 - 0.6 * math.exp(-0.3 * l)
        mod = jax.nn.silu(c) @ w_ada[l] + b_ada[l]
        sh1, sc1, g1, sh2, sc2, g2 = jnp.split(mod, 6, axis=-1)

        h = rmsnorm(x, norm1_w[l]) * (1.0 + sc1[:, None, :]) + sh1[:, None, :]
        proj = h @ w_att_in[l]
        splits = np.cumsum([COL_Q_LAT, COL_KV_LAT, COL_K_ROPE, COL_DQ, COL_DK, COL_DV, D_MODEL])
        q_lat, kv_lat, k_rope, dq, dk, dv, gate_a, gate_b = jnp.split(proj, splits, axis=-1)

        q = (rmsnorm(q_lat, mla_q_norm[l]) @ w_mla_qb[l]).reshape(B, S, MLA_HEADS, MLA_QK)
        kv = (rmsnorm(kv_lat, mla_kv_norm[l]) @ w_mla_kvb[l]).reshape(B, S, MLA_HEADS, MLA_NOPE + MLA_V)
        k_nope, v_a = kv[..., :MLA_NOPE], kv[..., MLA_NOPE:]
        k_r = jnp.broadcast_to(k_rope[:, :, None, :], (B, S, MLA_HEADS, MLA_ROPE))
        k = jnp.concatenate([k_nope, k_r], axis=-1)
        q = rmsnorm(q, mla_qk_norm_q[l])
        k = rmsnorm(k, mla_qk_norm_k[l])
        q = jnp.concatenate([q[..., :MLA_NOPE], rope(q[..., MLA_NOPE:], positions)], axis=-1)
        k = jnp.concatenate([k[..., :MLA_NOPE], rope(k[..., MLA_NOPE:], positions)], axis=-1)
        o_a = blocked_causal_attention(q, k, v_a, MLA_QK ** -0.5, lambda p: p)
        y_a = o_a.reshape(B, S, MLA_HEADS * MLA_V) @ w_mla_o[l]

        dq = rmsnorm(dq.reshape(B, S, 2 * DIFF_HEADS, DIFF_D), diff_q_norm[l])
        dk = rmsnorm(dk.reshape(B, S, 2 * DIFF_HEADS, DIFF_D), diff_k_norm[l])
        dq = jnp.concatenate([rope(dq[..., :DIFF_ROT], positions), dq[..., DIFF_ROT:]], axis=-1)
        dk = jnp.concatenate([rope(dk[..., :DIFF_ROT], positions), dk[..., DIFF_ROT:]], axis=-1)
        v_b = dv.reshape(B, S, DIFF_HEADS, DIFF_V)
        lam_p = diff_lambda[l].astype(jnp.float32)
        lam = (jnp.exp(jnp.sum(lam_p[0] * lam_p[1])) - jnp.exp(jnp.sum(lam_p[2] * lam_p[3]))
               + lambda_init)

        def diff_combine(p):
            p = p.reshape(p.shape[0], DIFF_HEADS, 2, p.shape[2], p.shape[3])
            return p[:, :, 0] - lam * p[:, :, 1]

        o_b = blocked_causal_attention(dq, dk, v_b, DIFF_D ** -0.5, diff_combine)
        o_b = rmsnorm(o_b, diff_subln[l]) * (1.0 - lambda_init)
        y_b = o_b.reshape(B, S, DIFF_HEADS * DIFF_V) @ w_diff_o[l]

        merged = jax.nn.sigmoid(gate_a) * y_a + jax.nn.sigmoid(gate_b) * y_b
        x = x + g1[:, None, :] * (merged @ w_att_out[l])

        h = rmsnorm(x, norm2_w[l]) * (1.0 + sc2[:, None, :]) + sh2[:, None, :]
        pq = (h @ w_peer_q[l]).reshape(B, S, PEER_HEADS, 2, PEER_HALF)
        sub = jnp.einsum('bshpd,hpnd->bshpn', pq, peer_keys[l]).astype(jnp.float32)
        sv, si = lax.top_k(sub, PEER_TOPK)
        cand = (sv[..., 0, :, None] + sv[..., 1, None, :]).reshape(B, S, PEER_HEADS, PEER_TOPK * PEER_TOPK)
        cidx = (si[..., 0, :, None] * PEER_NKEYS + si[..., 1, None, :]).reshape(B, S, PEER_HEADS, PEER_TOPK * PEER_TOPK)
        top, pos = lax.top_k(cand, PEER_TOPK)
        eidx = jnp.take_along_axis(cidx, pos, axis=-1)
        gates = jax.nn.softmax(top, axis=-1)
        T = B * S
        nch = T // PEER_CHUNK
        h_c = h.reshape(nch, PEER_CHUNK, D)
        i_c = eidx.reshape(nch, PEER_CHUNK, PEER_HEADS * PEER_TOPK)
        g_c = gates.reshape(nch, PEER_CHUNK, PEER_HEADS * PEER_TOPK)
        u_tab, v_tab = peer_u[l], peer_v[l]

        def peer_chunk(args):
            hc, ic, gc = args
            u = jnp.take(u_tab, ic, axis=0)
            a = jax.nn.gelu(jnp.einsum('cd,ckd->ck', hc, u).astype(jnp.float32), approximate=False)
            w = jnp.take(v_tab, ic, axis=0)
            return jnp.einsum('ck,ckd->cd', (gc * a).astype(hc.dtype), w)

        ffn = lax.map(peer_chunk, (h_c, i_c, g_c)).reshape(B, S, D)
        x = x + g2[:, None, :] * ffn
    return x
```

```python
import functools
import math

import jax
import jax.numpy as jnp
from jax import lax
from jax.experimental import pallas as pl
from jax.experimental.pallas import tpu as pltpu

F32 = jnp.float32
BF16 = jnp.bfloat16

ROPE_THETA = 500000.0
EPS = 1e-6
MLA_HEADS = 8
MLA_Q_LORA = 512
MLA_KV_LORA = 512
MLA_NOPE = 128
MLA_ROPE = 64
MLA_V = 128
MLA_QK = MLA_NOPE + MLA_ROPE
DIFF_HEADS = 8
DIFF_D = 64
DIFF_V = 2 * DIFF_D
DIFF_ROT = DIFF_D // 4
PEER_HEADS = 8
PEER_NKEYS = 128
PEER_HALF = 128
PEER_TOPK = 16

LANES = 128
SUBLANES = 8
VMEM_LIMIT_BYTES = 56 * 1024 * 1024
LOG2E = 1.4426950408889634
NEG_BIG = float(jnp.finfo(jnp.float32).min)
NEG_INF = float("-inf")
NT_DIMS = (((1,), (1,)), ((), ()))
TN_DIMS = (((0,), (0,)), ((), ()))

NRANK = PEER_TOPK + 1
RANK_ROWS = 24


def _params(*sem):
    return pltpu.CompilerParams(dimension_semantics=sem, vmem_limit_bytes=VMEM_LIMIT_BYTES)


def _rms(x, w):
    return x * lax.rsqrt(jnp.mean(x * x, axis=-1, keepdims=True) + EPS) * w


def _dot(a, b):
    return jnp.dot(a, b, preferred_element_type=F32)


def _mod_kernel(c_ref, w_ref, b_ref, o_ref):
    c = c_ref[...]
    a = (c * jax.nn.sigmoid(c)).astype(BF16)
    o_ref[...] = _dot(a, w_ref[...].astype(BF16)) + b_ref[...]


def _modulation(c, w, b):
    bsz, d = c.shape
    n = w.shape[1]
    tn = 1536
    return pl.pallas_call(
        _mod_kernel,
        grid=(n // tn,),
        in_specs=[pl.BlockSpec((bsz, d), lambda j: (0, 0)),
                  pl.BlockSpec((d, tn), lambda j: (0, j)),
                  pl.BlockSpec((1, tn), lambda j: (0, j))],
        out_specs=pl.BlockSpec((bsz, tn), lambda j: (0, j)),
        out_shape=jax.ShapeDtypeStruct((bsz, n), F32),
        compiler_params=_params("arbitrary"),
        name="adaln_mod",
    )(c, w, b.reshape(1, n))


def _trig_kernel(p_ref, inv_ref, cos_ref, sin_ref):
    ang = p_ref[...] * inv_ref[...]
    cos_ref[...] = jnp.cos(ang)
    sin_ref[...] = jnp.sin(ang)


def _rope_tables(pos_f32, r):
    t = pos_f32.shape[0]
    nf = r // 2
    inv = ROPE_THETA ** (-jnp.arange(0, r, 2, dtype=F32) / r)
    rep = LANES // nf
    rows = t // rep
    p = jnp.broadcast_to(pos_f32[:, None], (t, nf)).reshape(rows, LANES)
    inv_row = jnp.tile(inv, rep).reshape(1, LANES)
    tr = min(rows, 1024)
    cos, sin = pl.pallas_call(
        _trig_kernel,
        grid=(rows // tr,),
        in_specs=[pl.BlockSpec((tr, LANES), lambda i: (i, 0)),
                  pl.BlockSpec((1, LANES), lambda i: (0, 0))],
        out_specs=[pl.BlockSpec((tr, LANES), lambda i: (i, 0))] * 2,
        out_shape=[jax.ShapeDtypeStruct((rows, LANES), F32)] * 2,
        compiler_params=_params("arbitrary"),
        name="rope_tables",
    )(p, inv_row)
    return cos.reshape(t, nf), sin.reshape(t, nf)


def _mla_prep_kernel(x_ref, mod_ref, n1_ref, w1_ref, qn_ref, kvn_ref, wqn_ref, wqr_ref, wkn_ref,
                     wv_ref, gq_ref, gk_ref, c_ref, s_ref, h_ref, q_ref, k_ref, v_ref, *, qscale):
    x = x_ref[...]
    h = _rms(x, n1_ref[...]) * (1.0 + mod_ref[1:2, :]) + mod_ref[0:1, :]
    hb = h.astype(BF16)
    h_ref[...] = hb
    lat = _dot(hb, w1_ref[...])
    qn = _rms(lat[:, :MLA_Q_LORA], qn_ref[...]).astype(BF16)
    kvn = _rms(lat[:, MLA_Q_LORA:MLA_Q_LORA + MLA_KV_LORA], kvn_ref[...]).astype(BF16)
    kr = lat[:, MLA_Q_LORA + MLA_KV_LORA:]
    q_nope = _dot(qn, wqn_ref[...])
    q_rope = _dot(qn, wqr_ref[...])
    k_nope = _dot(kvn, wkn_ref[...])
    v_ref[...] = _dot(kvn, wv_ref[...]).astype(BF16)
    cos = c_ref[...]
    sin = s_ref[...]
    gq = gq_ref[...]
    gk = gk_ref[...]
    kr_ss = jnp.sum(kr * kr, axis=-1, keepdims=True)

    def rot(r):
        return r * cos + (pltpu.roll(r, LANES - 32, 1) + pltpu.roll(r, 32, 1)) * sin

    for hd in range(MLA_HEADS):
        lo, hi = hd * LANES, (hd + 1) * LANES
        a, r = q_nope[:, lo:hi], q_rope[:, lo:hi]
        ss = jnp.sum(a * a, axis=-1, keepdims=True) + jnp.sum(r * r, axis=-1, keepdims=True)
        inv = lax.rsqrt(ss * (1.0 / MLA_QK) + EPS)
        q_ref[:, 2 * lo:2 * lo + LANES] = (a * inv * gq[:, :LANES] * qscale).astype(BF16)
        q_ref[:, 2 * lo + LANES:2 * hi] = (rot(r * inv * gq[:, LANES:]) * qscale).astype(BF16)
        a = k_nope[:, lo:hi]
        ss = jnp.sum(a * a, axis=-1, keepdims=True) + kr_ss
        inv = lax.rsqrt(ss * (1.0 / MLA_QK) + EPS)
        k_ref[:, 2 * lo:2 * lo + LANES] = (a * inv * gk[:, :LANES]).astype(BF16)
        k_ref[:, 2 * lo + LANES:2 * hi] = rot(kr * inv * gk[:, LANES:]).astype(BF16)


def _mla_prep(xf, mod3, n1, w1, qnw, kvnw, wqn, wqr, wkn, wv, gq, gk, cos_t, sin_t, seq, tm):
    t, d = xf.shape
    per_b = seq // tm
    const = lambda i: (0, 0)
    row = lambda i: (i, 0)
    full = lambda a: pl.BlockSpec(a.shape, const)
    return pl.pallas_call(
        functools.partial(_mla_prep_kernel, qscale=MLA_QK ** -0.5 * LOG2E),
        grid=(t // tm,),
        in_specs=[pl.BlockSpec((tm, d), row),
                  pl.BlockSpec((None, 6, d), lambda i: (i // per_b, 0, 0)),
                  full(n1), full(w1), full(qnw), full(kvnw), full(wqn), full(wqr), full(wkn), full(wv),
                  full(gq), full(gk),
                  pl.BlockSpec((tm, LANES), row), pl.BlockSpec((tm, LANES), row)],
        out_specs=[pl.BlockSpec((tm, d), row),
                   pl.BlockSpec((tm, MLA_HEADS * 2 * LANES), row),
                   pl.BlockSpec((tm, MLA_HEADS * 2 * LANES), row),
                   pl.BlockSpec((tm, MLA_HEADS * MLA_V), row)],
        out_shape=[jax.ShapeDtypeStruct((t, d), BF16),
                   jax.ShapeDtypeStruct((t, MLA_HEADS * 2 * LANES), BF16),
                   jax.ShapeDtypeStruct((t, MLA_HEADS * 2 * LANES), BF16),
                   jax.ShapeDtypeStruct((t, MLA_HEADS * MLA_V), BF16)],
        compiler_params=_params("arbitrary"),
        name="mla_prep",
    )(xf, mod3, n1, w1, qnw, kvnw, wqn, wqr, wkn, wv, gq, gk, cos_t, sin_t)


def _diff_prep_kernel(h_ref, w_ref, g_ref, c_ref, sa_ref, sb_ref, o_ref):
    j = pl.program_id(1)
    d = _dot(h_ref[...], w_ref[...])

    @pl.when(j == 2)
    def _():
        o_ref[...] = d.astype(BF16)

    @pl.when(j < 2)
    def _():
        g = g_ref[...]
        cos, sa, sb = c_ref[...], sa_ref[...], sb_ref[...]
        lo = lax.broadcasted_iota(jnp.int32, (d.shape[0], LANES), 1) < DIFF_D
        for cb in range(d.shape[1] // LANES):
            blk = d[:, cb * LANES:(cb + 1) * LANES]
            sq = blk * blk
            s_lo = jnp.sum(jnp.where(lo, sq, 0.0), axis=-1, keepdims=True)
            s_hi = jnp.sum(jnp.where(lo, 0.0, sq), axis=-1, keepdims=True)
            inv = jnp.where(lo, lax.rsqrt(s_lo * (1.0 / DIFF_D) + EPS),
                            lax.rsqrt(s_hi * (1.0 / DIFF_D) + EPS))
            y = blk * inv * g
            y = y * cos + pltpu.roll(y, LANES - DIFF_ROT // 2, 1) * sa + pltpu.roll(y, DIFF_ROT // 2, 1) * sb
            o_ref[:, cb * LANES:(cb + 1) * LANES] = y.astype(BF16)


def _diff_prep(h, w2, g3, cos_t, sa_t, sb_t, tm):
    t, d = h.shape
    n = w2.shape[1]
    tn = n // 3
    row = lambda i, j: (i, 0)
    return pl.pallas_call(
        _diff_prep_kernel,
        grid=(t // tm, 3),
        in_specs=[pl.BlockSpec((tm, d), row),
                  pl.BlockSpec((d, tn), lambda i, j: (0, j)),
                  pl.BlockSpec((None, 1, LANES), lambda i, j: (j, 0, 0)),
                  pl.BlockSpec((tm, LANES), row), pl.BlockSpec((tm, LANES), row),
                  pl.BlockSpec((tm, LANES), row)],
        out_specs=pl.BlockSpec((tm, tn), lambda i, j: (i, j)),
        out_shape=jax.ShapeDtypeStruct((t, n), BF16),
        compiler_params=_params("arbitrary", "arbitrary"),
        name="diff_prep",
    )(h, w2, g3, cos_t, sa_t, sb_t)


def _causal_probs(q, k_ref, i, tq):
    sd = lax.dot_general(q, k_ref[i * tq:(i + 1) * tq, :], NT_DIMS, preferred_element_type=F32)
    rowi = lax.broadcasted_iota(jnp.int32, (tq, tq), 0)
    coli = lax.broadcasted_iota(jnp.int32, (tq, tq), 1)
    sd = jnp.where(coli <= rowi, sd, NEG_BIG)
    m = jnp.max(sd, axis=-1, keepdims=True)
    sp = None
    if i > 0:
        sp = lax.dot_general(q, k_ref[0:i * tq, :], NT_DIMS, preferred_element_type=F32)
        m = jnp.maximum(m, jnp.max(sp, axis=-1, keepdims=True))
    pd = jnp.exp2(sd - m)
    l = jnp.sum(pd, axis=-1, keepdims=True)
    pp = None
    if i > 0:
        pp = jnp.exp2(sp - m)
        l = l + jnp.sum(pp, axis=-1, keepdims=True)
    return pd, pp, l


def _pv(pd, pp, v_ref, i, tq):
    o = _dot(pd.astype(BF16), v_ref[i * tq:(i + 1) * tq, :])
    if pp is not None:
        o = o + _dot(pp.astype(BF16), v_ref[0:i * tq, :])
    return o


def _mla_attn_kernel(q_ref, k_ref, v_ref, o_ref, *, tq):
    for i in range(q_ref.shape[0] // tq):
        pd, pp, l = _causal_probs(q_ref[i * tq:(i + 1) * tq, :], k_ref, i, tq)
        o = _pv(pd, pp, v_ref, i, tq)
        o_ref[i * tq:(i + 1) * tq, :] = (o * (1.0 / l)).astype(BF16)


def _mla_attn(q, k, v, tq):
    b, s, _ = q.shape
    qk = lambda bi, h: (bi, 0, h)
    return pl.pallas_call(
        functools.partial(_mla_attn_kernel, tq=tq),
        grid=(b, MLA_HEADS),
        in_specs=[pl.BlockSpec((None, s, 2 * LANES), qk), pl.BlockSpec((None, s, 2 * LANES), qk),
                  pl.BlockSpec((None, s, MLA_V), qk)],
        out_specs=pl.BlockSpec((None, s, MLA_V), qk),
        out_shape=jax.ShapeDtypeStruct((b, s, MLA_HEADS * MLA_V), BF16),
        compiler_params=_params("arbitrary", "arbitrary"),
        name="mla_attn",
    )(q, k, v)


def _diff_attn_kernel(lam_ref, g_ref, q_ref, k_ref, v_ref, o_ref, *, tq, lambda_init):
    lp = lam_ref[...]
    lam = (jnp.exp(jnp.sum(lp[0:1] * lp[1:2], axis=-1, keepdims=True))
           - jnp.exp(jnp.sum(lp[2:3] * lp[3:4], axis=-1, keepdims=True)) + lambda_init)
    g = g_ref[...] * (1.0 - lambda_init)
    lo = lax.broadcasted_iota(jnp.int32, (tq, LANES), 1) < DIFF_D
    for i in range(q_ref.shape[0] // tq):
        q = q_ref[i * tq:(i + 1) * tq, :]
        zero = jnp.zeros_like(q)
        pd1, pp1, l1 = _causal_probs(jnp.where(lo, q, zero), k_ref, i, tq)
        pd2, pp2, l2 = _causal_probs(jnp.where(lo, zero, q), k_ref, i, tq)
        c1 = 1.0 / l1
        c2 = lam / l2
        wd = pd1 * c1 - pd2 * c2
        wp = None if pp1 is None else pp1 * c1 - pp2 * c2
        o = _pv(wd, wp, v_ref, i, tq)
        o_ref[i * tq:(i + 1) * tq, :] = _rms(o, g).astype(BF16)


def _diff_attn(qkv, lam_p, subln, tq, lambda_init):
    b, s, _ = qkv.shape
    return pl.pallas_call(
        functools.partial(_diff_attn_kernel, tq=tq, lambda_init=lambda_init),
        grid=(b, DIFF_HEADS),
        in_specs=[pl.BlockSpec(lam_p.shape, lambda bi, h: (0, 0)),
                  pl.BlockSpec(subln.shape, lambda bi, h: (0, 0)),
                  pl.BlockSpec((None, s, LANES), lambda bi, h: (bi, 0, h)),
                  pl.BlockSpec((None, s, LANES), lambda bi, h: (bi, 0, DIFF_HEADS + h)),
                  pl.BlockSpec((None, s, LANES), lambda bi, h: (bi, 0, 2 * DIFF_HEADS + h))],
        out_specs=pl.BlockSpec((None, s, DIFF_V), lambda bi, h: (bi, 0, h)),
        out_shape=jax.ShapeDtypeStruct((b, s, DIFF_HEADS * DIFF_V), BF16),
        compiler_params=_params("arbitrary", "arbitrary"),
        name="diff_attn",
    )(lam_p, subln, qkv, qkv, qkv)


def _merge_kernel(h_ref, oa_ref, ob_ref, wga_ref, wgb_ref, woa_ref, wob_ref, o_ref):
    h = h_ref[...]
    ya = _dot(oa_ref[...], woa_ref[...])
    yb = _dot(ob_ref[...], wob_ref[...])
    ga = jax.nn.sigmoid(_dot(h, wga_ref[...]))
    gb = jax.nn.sigmoid(_dot(h, wgb_ref[...]))
    o_ref[...] = (ga * ya + gb * yb).astype(BF16)


def _merge(h, oa, ob, wg, woa, wob, tm, tn):
    t, d = h.shape
    nj = d // tn
    row = lambda i, j: (i, 0)
    col = lambda i, j: (0, j)
    return pl.pallas_call(
        _merge_kernel,
        grid=(t // tm, nj),
        in_specs=[pl.BlockSpec((tm, d), row),
                  pl.BlockSpec((tm, oa.shape[1]), row), pl.BlockSpec((tm, ob.shape[1]), row),
                  pl.BlockSpec((d, tn), col), pl.BlockSpec((d, tn), lambda i, j: (0, nj + j)),
                  pl.BlockSpec((woa.shape[0], tn), col), pl.BlockSpec((wob.shape[0], tn), col)],
        out_specs=pl.BlockSpec((tm, tn), lambda i, j: (i, j)),
        out_shape=jax.ShapeDtypeStruct((t, d), BF16),
        compiler_params=_params("arbitrary", "arbitrary"),
        name="gated_merge",
    )(h, oa, ob, wg, wg, woa, wob)


def _outproj_kernel(m_ref, w_ref, x_ref, mod_ref, n2_ref, x1_ref, h2_ref):
    x1 = x_ref[...] + mod_ref[2:3, :] * _dot(m_ref[...], w_ref[...])
    x1_ref[...] = x1
    h2_ref[...] = (_rms(x1, n2_ref[...]) * (1.0 + mod_ref[4:5, :]) + mod_ref[3:4, :]).astype(BF16)


def _outproj(merged, wout, xf, mod3, n2, seq, tm):
    t, d = xf.shape
    per_b = seq // tm
    row = lambda i: (i, 0)
    return pl.pallas_call(
        _outproj_kernel,
        grid=(t // tm,),
        in_specs=[pl.BlockSpec((tm, d), row), pl.BlockSpec((d, d), lambda i: (0, 0)),
                  pl.BlockSpec((tm, d), row),
                  pl.BlockSpec((None, 6, d), lambda i: (i // per_b, 0, 0)),
                  pl.BlockSpec((1, d), lambda i: (0, 0))],
        out_specs=[pl.BlockSpec((tm, d), row), pl.BlockSpec((tm, d), row)],
        out_shape=[jax.ShapeDtypeStruct((t, d), F32), jax.ShapeDtypeStruct((t, d), BF16)],
        compiler_params=_params("arbitrary"),
        name="outproj_norm2",
    )(merged, wout, xf, mod3, n2)


def _top_values(work, out_ref):
    out_ref[...] = jnp.full(out_ref.shape, NEG_INF, F32)
    for r in range(NRANK):
        m = jnp.max(work, axis=0, keepdims=True)
        out_ref[r:r + 1, :] = m
        work = jnp.where(work == m, NEG_INF, work)


def _peer_route_kernel(h_ref, wqt_ref, keys_ref, r_ref, va_ref, vb_ref, vc_ref):
    tm = h_ref.shape[0]
    pq_t = lax.dot_general(wqt_ref[...], h_ref[...], NT_DIMS, preferred_element_type=F32)
    rowi = lax.broadcasted_iota(jnp.int32, (SUBLANES, tm), 0)
    for hd in range(PEER_HEADS):
        sub = []
        for p, vref in ((0, va_ref), (1, vb_ref)):
            g = 2 * hd + p
            s = _dot(keys_ref[g], pq_t[g * PEER_HALF:(g + 1) * PEER_HALF, :].astype(BF16))
            sub.append(s)
            _top_values(s, vref)
        s1, s2 = sub
        slabs = [va_ref[0:1, :] + vb_ref[...]]
        for k in range(1, SUBLANES):
            n_l = NRANK // (k + 1)
            sl = va_ref[k:k + 1, :] + vb_ref[0:SUBLANES, :]
            slabs.append(sl if n_l >= SUBLANES else jnp.where(rowi < n_l, sl, NEG_INF))
        slabs.append(va_ref[SUBLANES:, :] + vb_ref[0:1, :])
        _top_values(jnp.concatenate(slabs, axis=0), vc_ref)
        c0 = vc_ref[0:1, :]
        z = jnp.sum(jnp.exp(vc_ref[0:PEER_TOPK, :] - c0), axis=0, keepdims=True)
        thr = 0.5 * (vc_ref[PEER_TOPK - 1:PEER_TOPK, :] + vc_ref[PEER_TOPK:PEER_TOPK + 1, :])
        r_ref[hd, 0] = s2
        r_ref[hd, 1] = jnp.exp(s2 - vb_ref[0:1, :]) * (1.0 / z)
        r_ref[hd, 2] = thr - s1
        r_ref[hd, 3] = jnp.exp(s1 - va_ref[0:1, :])


def _peer_route(h2, wqt, keys_b, tm):
    t, d = h2.shape
    return pl.pallas_call(
        _peer_route_kernel,
        grid=(t // tm,),
        in_specs=[pl.BlockSpec((tm, d), lambda i: (i, 0)),
                  pl.BlockSpec(wqt.shape, lambda i: (0, 0)),
                  pl.BlockSpec(keys_b.shape, lambda i: (0, 0, 0))],
        out_specs=pl.BlockSpec((PEER_HEADS, 4, PEER_NKEYS, tm), lambda i: (0, 0, 0, i)),
        out_shape=jax.ShapeDtypeStruct((PEER_HEADS, 4, PEER_NKEYS, t), F32),
        scratch_shapes=[pltpu.VMEM((RANK_ROWS, tm), F32)] * 3,
        compiler_params=_params("arbitrary"),
        name="peer_route",
    )(h2, wqt, keys_b)


def _peer_dense_kernel(h_ref, u_ref, v_ref, rj_ref, ri_ref, x1_ref, mod_ref, o_ref, w_ref):
    e = pl.program_id(1)
    te, tm = w_ref.shape
    n_i = te // PEER_NKEYS

    @pl.when(e == 0)
    def _():
        o_ref[...] = jnp.zeros(o_ref.shape, F32)

    a_t = lax.dot_general(u_ref[...], h_ref[...], NT_DIMS, preferred_element_type=F32)
    for ii in range(n_i):
        for cb in range(tm // LANES):
            cols = pl.ds(cb * LANES, LANES)
            gate = jnp.zeros((PEER_NKEYS, LANES), F32)
            for hd in range(PEER_HEADS):
                thr = ri_ref[hd, 0, ii:ii + 1, cols]
                e1 = ri_ref[hd, 1, ii:ii + 1, cols]
                gate = gate + jnp.where(rj_ref[hd, 0, :, cols] >= thr, rj_ref[hd, 1, :, cols], 0.0) * e1
            a = a_t[ii * PEER_NKEYS:(ii + 1) * PEER_NKEYS, cb * LANES:(cb + 1) * LANES]
            gelu = 0.5 * a * (1.0 + lax.erf(a * (2.0 ** -0.5)))
            w_ref[ii * PEER_NKEYS:(ii + 1) * PEER_NKEYS, cols] = (gate * gelu).astype(BF16)
    o_ref[...] += lax.dot_general(w_ref[...], v_ref[...], TN_DIMS, preferred_element_type=F32)

    @pl.when(e == pl.num_programs(1) - 1)
    def _():
        o_ref[...] = x1_ref[...] + mod_ref[5:6, :] * o_ref[...]


def _peer_dense(h2, u_b, v_b, route, x1, mod3, seq, tm, te):
    t, d = h2.shape
    n_e = u_b.shape[0]
    per_b = seq // tm
    row = lambda i, e: (i, 0)
    return pl.pallas_call(
        _peer_dense_kernel,
        grid=(t // tm, n_e // te),
        in_specs=[pl.BlockSpec((tm, d), row),
                  pl.BlockSpec((te, d), lambda i, e: (e, 0)),
                  pl.BlockSpec((te, d), lambda i, e: (e, 0)),
                  pl.BlockSpec((PEER_HEADS, 2, PEER_NKEYS, tm), lambda i, e: (0, 0, 0, i)),
                  pl.BlockSpec((PEER_HEADS, 2, te // PEER_NKEYS, tm), lambda i, e: (0, 1, e, i)),
                  pl.BlockSpec((tm, d), row),
                  pl.BlockSpec((None, 6, d), lambda i, e: (i // per_b, 0, 0))],
        out_specs=pl.BlockSpec((tm, d), row),
        out_shape=jax.ShapeDtypeStruct((t, d), F32),
        scratch_shapes=[pltpu.VMEM((te, tm), BF16)],
        compiler_params=_params("arbitrary", "arbitrary"),
        name="peer_dense",
    )(h2, u_b, v_b, route, route, x1, mod3)


def _pad_lanes(a, width):
    return jnp.concatenate([a, jnp.zeros(a.shape[:-1] + (width - a.shape[-1],), a.dtype)], axis=-1)


def kernel(x, c, positions, norm1_w, norm2_w, w_ada, b_ada, w_att_in, mla_q_norm, w_mla_qb, mla_kv_norm,
           w_mla_kvb, mla_qk_norm_q, mla_qk_norm_k, w_mla_o, diff_q_norm, diff_k_norm, diff_lambda,
           diff_subln, w_diff_o, w_att_out, w_peer_q, peer_keys, peer_u, peer_v):
    bsz, seq, d = x.shape
    t = bsz * seq
    tm = min(512, seq)
    tq = min(256, seq)
    xf = x.reshape(t, d)
    pos = positions.reshape(t).astype(F32)

    cos_m, sin_m = _rope_tables(pos, MLA_ROPE)
    z64 = jnp.zeros((t, LANES - MLA_ROPE), F32)
    mla_cos = jnp.concatenate([cos_m, cos_m, z64], axis=-1)
    mla_sin = jnp.concatenate([-sin_m, sin_m, z64], axis=-1)
    cos_d, sin_d = _rope_tables(pos, DIFF_ROT)
    half = DIFF_ROT // 2
    rest = DIFF_D - DIFF_ROT
    z8 = jnp.zeros((t, half), F32)
    diff_cos = jnp.tile(jnp.concatenate([cos_d, cos_d, jnp.ones((t, rest), F32)], axis=-1), (1, 2))
    diff_sa = jnp.tile(jnp.concatenate([-sin_d, z8, jnp.zeros((t, rest), F32)], axis=-1), (1, 2))
    diff_sb = jnp.tile(jnp.concatenate([z8, sin_d, jnp.zeros((t, rest), F32)], axis=-1), (1, 2))

    n_lat = MLA_Q_LORA + MLA_KV_LORA + MLA_ROPE
    n_diff = 3 * DIFF_HEADS * DIFF_V
    for l in range(w_ada.shape[0]):
        lambda_init = 0.8 - 0.6 * math.exp(-0.3 * l)
        w_in = w_att_in[l].astype(BF16)
        w1 = _pad_lanes(w_in[:, :n_lat], n_lat + LANES - MLA_ROPE)
        w2 = w_in[:, n_lat:n_lat + n_diff]
        wg = w_in[:, n_lat + n_diff:]
        wqb = w_mla_qb[l].astype(BF16).reshape(MLA_Q_LORA, MLA_HEADS, MLA_QK)
        wqn = wqb[:, :, :MLA_NOPE].reshape(MLA_Q_LORA, MLA_HEADS * MLA_NOPE)
        wqr = _pad_lanes(wqb[:, :, MLA_NOPE:], LANES).reshape(MLA_Q_LORA, MLA_HEADS * LANES)
        wkvb = w_mla_kvb[l].astype(BF16).reshape(MLA_KV_LORA, MLA_HEADS, MLA_NOPE + MLA_V)
        wkn = wkvb[:, :, :MLA_NOPE].reshape(MLA_KV_LORA, MLA_HEADS * MLA_NOPE)
        wv = wkvb[:, :, MLA_NOPE:].reshape(MLA_KV_LORA, MLA_HEADS * MLA_V)
        gq = _pad_lanes(mla_qk_norm_q[l].reshape(1, MLA_QK), 2 * LANES)
        gk = _pad_lanes(mla_qk_norm_k[l].reshape(1, MLA_QK), 2 * LANES)
        g3 = jnp.stack([jnp.tile(diff_q_norm[l], 2) * (DIFF_D ** -0.5 * LOG2E),
                        jnp.tile(diff_k_norm[l], 2),
                        jnp.ones((LANES,), F32)]).reshape(3, 1, LANES)

        mod3 = _modulation(c, w_ada[l], b_ada[l]).reshape(bsz, 6, d)

        h, q, k, v = _mla_prep(xf, mod3, norm1_w[l].reshape(1, d), w1,
                               mla_q_norm[l].reshape(1, -1), mla_kv_norm[l].reshape(1, -1),
                               wqn, wqr, wkn, wv, gq, gk, mla_cos, mla_sin, seq, tm)
        dqkv = _diff_prep(h, w2, g3, diff_cos, diff_sa, diff_sb, tm)
        o_a = _mla_attn(q.reshape(bsz, seq, -1), k.reshape(bsz, seq, -1), v.reshape(bsz, seq, -1), tq)
        o_b = _diff_attn(dqkv.reshape(bsz, seq, -1), diff_lambda[l], diff_subln[l].reshape(1, DIFF_V),
                         tq, lambda_init)
        merged = _merge(h, o_a.reshape(t, -1), o_b.reshape(t, -1), wg, w_mla_o[l].astype(BF16),
                        w_diff_o[l].astype(BF16), tm, 512)
        x1, h2 = _outproj(merged, w_att_out[l].astype(BF16), xf, mod3, norm2_w[l].reshape(1, d), seq, tm)

        route = _peer_route(h2, w_peer_q[l].T.astype(BF16),
                            peer_keys[l].reshape(2 * PEER_HEADS, PEER_NKEYS, PEER_HALF).astype(BF16),
                            min(256, seq))
        xf = _peer_dense(h2, peer_u[l].astype(BF16), peer_v[l].astype(BF16), route, x1, mod3, seq, tm, 1024)
    return xf.reshape(bsz, seq, d)
```

```python
import functools
import math

import jax
import jax.numpy as jnp
from jax import lax
from jax.experimental import pallas as pl
from jax.experimental.pallas import tpu as pltpu

F32 = jnp.float32
BF16 = jnp.bfloat16

ROPE_THETA = 500000.0
EPS = 1e-6
MLA_HEADS = 8
MLA_Q_LORA = 512
MLA_KV_LORA = 512
MLA_NOPE = 128
MLA_ROPE = 64
MLA_V = 128
MLA_QK = MLA_NOPE + MLA_ROPE
DIFF_HEADS = 8
DIFF_D = 64
DIFF_V = 2 * DIFF_D
DIFF_ROT = DIFF_D // 4
PEER_HEADS = 8
PEER_NKEYS = 128
PEER_HALF = 128
PEER_TOPK = 16

LANES = 128
SUBLANES = 8
VMEM_LIMIT_BYTES = 56 * 1024 * 1024
LOG2E = 1.4426950408889634
NEG_BIG = float(jnp.finfo(jnp.float32).min)
NEG_INF = float("-inf")
NT_DIMS = (((1,), (1,)), ((), ()))
TN_DIMS = (((0,), (0,)), ((), ()))

NRANK = PEER_TOPK + 1
RANK_ROWS = 24


def _params(*sem, flags=None):
    return pltpu.CompilerParams(dimension_semantics=sem, vmem_limit_bytes=VMEM_LIMIT_BYTES, flags=flags)


def _rms(x, w):
    return x * lax.rsqrt(jnp.mean(x * x, axis=-1, keepdims=True) + EPS) * w


def _dot(a, b):
    return jnp.dot(a, b, preferred_element_type=F32)


def _mod_kernel(c_ref, w_ref, b_ref, o_ref):
    c = c_ref[...]
    a = (c * jax.nn.sigmoid(c)).astype(BF16)
    o_ref[...] = _dot(a, w_ref[...].astype(BF16)) + b_ref[...]


def _modulation(c, w, b):
    bsz, d = c.shape
    n = w.shape[1]
    tn = 1536
    return pl.pallas_call(
        _mod_kernel,
        grid=(n // tn,),
        in_specs=[pl.BlockSpec((bsz, d), lambda j: (0, 0)),
                  pl.BlockSpec((d, tn), lambda j: (0, j)),
                  pl.BlockSpec((1, tn), lambda j: (0, j))],
        out_specs=pl.BlockSpec((bsz, tn), lambda j: (0, j)),
        out_shape=jax.ShapeDtypeStruct((bsz, n), F32),
        compiler_params=_params("arbitrary"),
        name="adaln_mod",
    )(c, w, b.reshape(1, n))


def _trig_kernel(p_ref, inv_ref, cos_ref, sin_ref):
    ang = p_ref[...] * inv_ref[...]
    cos_ref[...] = jnp.cos(ang)
    sin_ref[...] = jnp.sin(ang)


def _rope_tables(pos_f32, r):
    t = pos_f32.shape[0]
    nf = r // 2
    inv = ROPE_THETA ** (-jnp.arange(0, r, 2, dtype=F32) / r)
    rep = LANES // nf
    rows = t // rep
    p = jnp.broadcast_to(pos_f32[:, None], (t, nf)).reshape(rows, LANES)
    inv_row = jnp.tile(inv, rep).reshape(1, LANES)
    tr = min(rows, 1024)
    cos, sin = pl.pallas_call(
        _trig_kernel,
        grid=(rows // tr,),
        in_specs=[pl.BlockSpec((tr, LANES), lambda i: (i, 0)),
                  pl.BlockSpec((1, LANES), lambda i: (0, 0))],
        out_specs=[pl.BlockSpec((tr, LANES), lambda i: (i, 0))] * 2,
        out_shape=[jax.ShapeDtypeStruct((rows, LANES), F32)] * 2,
        compiler_params=_params("arbitrary"),
        name="rope_tables",
    )(p, inv_row)
    return cos.reshape(t, nf), sin.reshape(t, nf)


def _mla_prep_kernel(x_ref, mod_ref, n1_ref, w1_ref, qn_ref, kvn_ref, wqn_ref, wqr_ref, wkn_ref,
                     wv_ref, gq_ref, gk_ref, c_ref, s_ref, h_ref, q_ref, k_ref, v_ref, *, qscale):
    x = x_ref[...]
    h = _rms(x, n1_ref[...]) * (1.0 + mod_ref[1:2, :]) + mod_ref[0:1, :]
    hb = h.astype(BF16)
    h_ref[...] = hb
    lat = _dot(hb, w1_ref[...])
    qn = _rms(lat[:, :MLA_Q_LORA], qn_ref[...]).astype(BF16)
    kvn = _rms(lat[:, MLA_Q_LORA:MLA_Q_LORA + MLA_KV_LORA], kvn_ref[...]).astype(BF16)
    kr = lat[:, MLA_Q_LORA + MLA_KV_LORA:]
    q_nope = _dot(qn, wqn_ref[...])
    q_rope = _dot(qn, wqr_ref[...])
    k_nope = _dot(kvn, wkn_ref[...])
    v_ref[...] = _dot(kvn, wv_ref[...]).astype(BF16)
    cos = c_ref[...]
    sin = s_ref[...]
    gq = gq_ref[...]
    gk = gk_ref[...]
    kr_ss = jnp.sum(kr * kr, axis=-1, keepdims=True)

    def rot(r):
        return r * cos + (pltpu.roll(r, LANES - 32, 1) + pltpu.roll(r, 32, 1)) * sin

    for hd in range(MLA_HEADS):
        lo, hi = hd * LANES, (hd + 1) * LANES
        a, r = q_nope[:, lo:hi], q_rope[:, lo:hi]
        ss = jnp.sum(a * a, axis=-1, keepdims=True) + jnp.sum(r * r, axis=-1, keepdims=True)
        inv = lax.rsqrt(ss * (1.0 / MLA_QK) + EPS)
        q_ref[:, 2 * lo:2 * lo + LANES] = (a * inv * gq[:, :LANES] * qscale).astype(BF16)
        q_ref[:, 2 * lo + LANES:2 * hi] = (rot(r * inv * gq[:, LANES:]) * qscale).astype(BF16)
        a = k_nope[:, lo:hi]
        ss = jnp.sum(a * a, axis=-1, keepdims=True) + kr_ss
        inv = lax.rsqrt(ss * (1.0 / MLA_QK) + EPS)
        k_ref[:, 2 * lo:2 * lo + LANES] = (a * inv * gk[:, :LANES]).astype(BF16)
        k_ref[:, 2 * lo + LANES:2 * hi] = rot(kr * inv * gk[:, LANES:]).astype(BF16)


def _mla_prep(xf, mod3, n1, w1, qnw, kvnw, wqn, wqr, wkn, wv, gq, gk, cos_t, sin_t, seq, tm):
    t, d = xf.shape
    per_b = seq // tm
    const = lambda i: (0, 0)
    row = lambda i: (i, 0)
    full = lambda a: pl.BlockSpec(a.shape, const)
    return pl.pallas_call(
        functools.partial(_mla_prep_kernel, qscale=MLA_QK ** -0.5 * LOG2E),
        grid=(t // tm,),
        in_specs=[pl.BlockSpec((tm, d), row),
                  pl.BlockSpec((None, 6, d), lambda i: (i // per_b, 0, 0)),
                  full(n1), full(w1), full(qnw), full(kvnw), full(wqn), full(wqr), full(wkn), full(wv),
                  full(gq), full(gk),
                  pl.BlockSpec((tm, LANES), row), pl.BlockSpec((tm, LANES), row)],
        out_specs=[pl.BlockSpec((tm, d), row),
                   pl.BlockSpec((tm, MLA_HEADS * 2 * LANES), row),
                   pl.BlockSpec((tm, MLA_HEADS * 2 * LANES), row),
                   pl.BlockSpec((tm, MLA_HEADS * MLA_V), row)],
        out_shape=[jax.ShapeDtypeStruct((t, d), BF16),
                   jax.ShapeDtypeStruct((t, MLA_HEADS * 2 * LANES), BF16),
                   jax.ShapeDtypeStruct((t, MLA_HEADS * 2 * LANES), BF16),
                   jax.ShapeDtypeStruct((t, MLA_HEADS * MLA_V), BF16)],
        compiler_params=_params("arbitrary"),
        name="mla_prep",
    )(xf, mod3, n1, w1, qnw, kvnw, wqn, wqr, wkn, wv, gq, gk, cos_t, sin_t)


def _diff_prep_kernel(h_ref, w_ref, g_ref, c_ref, sa_ref, sb_ref, o_ref):
    j = pl.program_id(1)
    d = _dot(h_ref[...], w_ref[...])

    @pl.when(j == 2)
    def _():
        o_ref[...] = d.astype(BF16)

    @pl.when(j < 2)
    def _():
        g = g_ref[...]
        cos, sa, sb = c_ref[...], sa_ref[...], sb_ref[...]
        lo = lax.broadcasted_iota(jnp.int32, (d.shape[0], LANES), 1) < DIFF_D
        for cb in range(d.shape[1] // LANES):
            blk = d[:, cb * LANES:(cb + 1) * LANES]
            sq = blk * blk
            s_lo = jnp.sum(jnp.where(lo, sq, 0.0), axis=-1, keepdims=True)
            s_hi = jnp.sum(jnp.where(lo, 0.0, sq), axis=-1, keepdims=True)
            inv = jnp.where(lo, lax.rsqrt(s_lo * (1.0 / DIFF_D) + EPS),
                            lax.rsqrt(s_hi * (1.0 / DIFF_D) + EPS))
            y = blk * inv * g
            y = y * cos + pltpu.roll(y, LANES - DIFF_ROT // 2, 1) * sa + pltpu.roll(y, DIFF_ROT // 2, 1) * sb
            o_ref[:, cb * LANES:(cb + 1) * LANES] = y.astype(BF16)


def _diff_prep(h, w2, g3, cos_t, sa_t, sb_t, tm):
    t, d = h.shape
    n = w2.shape[1]
    tn = n // 3
    row = lambda i, j: (i, 0)
    return pl.pallas_call(
        _diff_prep_kernel,
        grid=(t // tm, 3),
        in_specs=[pl.BlockSpec((tm, d), row),
                  pl.BlockSpec((d, tn), lambda i, j: (0, j)),
                  pl.BlockSpec((None, 1, LANES), lambda i, j: (j, 0, 0)),
                  pl.BlockSpec((tm, LANES), row), pl.BlockSpec((tm, LANES), row),
                  pl.BlockSpec((tm, LANES), row)],
        out_specs=pl.BlockSpec((tm, tn), lambda i, j: (i, j)),
        out_shape=jax.ShapeDtypeStruct((t, n), BF16),
        compiler_params=_params("arbitrary", "arbitrary"),
        name="diff_prep",
    )(h, w2, g3, cos_t, sa_t, sb_t)


def _causal_probs(q, k_ref, i, tq):
    sd = lax.dot_general(q, k_ref[i * tq:(i + 1) * tq, :], NT_DIMS, preferred_element_type=F32)
    rowi = lax.broadcasted_iota(jnp.int32, (tq, tq), 0)
    coli = lax.broadcasted_iota(jnp.int32, (tq, tq), 1)
    sd = jnp.where(coli <= rowi, sd, NEG_BIG)
    m = jnp.max(sd, axis=-1, keepdims=True)
    sp = None
    if i > 0:
        sp = lax.dot_general(q, k_ref[0:i * tq, :], NT_DIMS, preferred_element_type=F32)
        m = jnp.maximum(m, jnp.max(sp, axis=-1, keepdims=True))
    pd = jnp.exp2(sd - m)
    l = jnp.sum(pd, axis=-1, keepdims=True)
    pp = None
    if i > 0:
        pp = jnp.exp2(sp - m)
        l = l + jnp.sum(pp, axis=-1, keepdims=True)
    return pd, pp, l


def _pv(pd, pp, v_ref, i, tq):
    o = _dot(pd.astype(BF16), v_ref[i * tq:(i + 1) * tq, :])
    if pp is not None:
        o = o + _dot(pp.astype(BF16), v_ref[0:i * tq, :])
    return o


def _mla_attn_kernel(q_ref, k_ref, v_ref, o_ref, *, tq):
    for i in range(q_ref.shape[0] // tq):
        pd, pp, l = _causal_probs(q_ref[i * tq:(i + 1) * tq, :], k_ref, i, tq)
        o = _pv(pd, pp, v_ref, i, tq)
        o_ref[i * tq:(i + 1) * tq, :] = (o * (1.0 / l)).astype(BF16)


def _mla_attn(q, k, v, tq):
    b, s, _ = q.shape
    qk = lambda bi, h: (bi, 0, h)
    return pl.pallas_call(
        functools.partial(_mla_attn_kernel, tq=tq),
        grid=(b, MLA_HEADS),
        in_specs=[pl.BlockSpec((None, s, 2 * LANES), qk), pl.BlockSpec((None, s, 2 * LANES), qk),
                  pl.BlockSpec((None, s, MLA_V), qk)],
        out_specs=pl.BlockSpec((None, s, MLA_V), qk),
        out_shape=jax.ShapeDtypeStruct((b, s, MLA_HEADS * MLA_V), BF16),
        compiler_params=_params("arbitrary", "arbitrary"),
        name="mla_attn",
    )(q, k, v)


def _diff_attn_kernel(lam_ref, g_ref, q_ref, k_ref, v_ref, o_ref, *, tq, lambda_init):
    lp = lam_ref[...]
    lam = (jnp.exp(jnp.sum(lp[0:1] * lp[1:2], axis=-1, keepdims=True))
           - jnp.exp(jnp.sum(lp[2:3] * lp[3:4], axis=-1, keepdims=True)) + lambda_init)
    g = g_ref[...] * (1.0 - lambda_init)
    lo = lax.broadcasted_iota(jnp.int32, (tq, LANES), 1) < DIFF_D
    for i in range(q_ref.shape[0] // tq):
        q = q_ref[i * tq:(i + 1) * tq, :]
        zero = jnp.zeros_like(q)
        pd1, pp1, l1 = _causal_probs(jnp.where(lo, q, zero), k_ref, i, tq)
        pd2, pp2, l2 = _causal_probs(jnp.where(lo, zero, q), k_ref, i, tq)
        c1 = 1.0 / l1
        c2 = lam / l2
        wd = pd1 * c1 - pd2 * c2
        wp = None if pp1 is None else pp1 * c1 - pp2 * c2
        o = _pv(wd, wp, v_ref, i, tq)
        o_ref[i * tq:(i + 1) * tq, :] = _rms(o, g).astype(BF16)


def _diff_attn(qkv, lam_p, subln, tq, lambda_init):
    b, s, _ = qkv.shape
    return pl.pallas_call(
        functools.partial(_diff_attn_kernel, tq=tq, lambda_init=lambda_init),
        grid=(b, DIFF_HEADS),
        in_specs=[pl.BlockSpec(lam_p.shape, lambda bi, h: (0, 0)),
                  pl.BlockSpec(subln.shape, lambda bi, h: (0, 0)),
                  pl.BlockSpec((None, s, LANES), lambda bi, h: (bi, 0, h)),
                  pl.BlockSpec((None, s, LANES), lambda bi, h: (bi, 0, DIFF_HEADS + h)),
                  pl.BlockSpec((None, s, LANES), lambda bi, h: (bi, 0, 2 * DIFF_HEADS + h))],
        out_specs=pl.BlockSpec((None, s, DIFF_V), lambda bi, h: (bi, 0, h)),
        out_shape=jax.ShapeDtypeStruct((b, s, DIFF_HEADS * DIFF_V), BF16),
        compiler_params=_params("arbitrary", "arbitrary"),
        name="diff_attn",
    )(lam_p, subln, qkv, qkv, qkv)


def _merge_kernel(h_ref, oa_ref, ob_ref, wga_ref, wgb_ref, woa_ref, wob_ref, o_ref):
    h = h_ref[...]
    ya = _dot(oa_ref[...], woa_ref[...])
    yb = _dot(ob_ref[...], wob_ref[...])
    ga = jax.nn.sigmoid(_dot(h, wga_ref[...]))
    gb = jax.nn.sigmoid(_dot(h, wgb_ref[...]))
    o_ref[...] = (ga * ya + gb * yb).astype(BF16)


def _merge(h, oa, ob, wg, woa, wob, tm, tn):
    t, d = h.shape
    nj = d // tn
    row = lambda i, j: (i, 0)
    col = lambda i, j: (0, j)
    return pl.pallas_call(
        _merge_kernel,
        grid=(t // tm, nj),
        in_specs=[pl.BlockSpec((tm, d), row),
                  pl.BlockSpec((tm, oa.shape[1]), row), pl.BlockSpec((tm, ob.shape[1]), row),
                  pl.BlockSpec((d, tn), col), pl.BlockSpec((d, tn), lambda i, j: (0, nj + j)),
                  pl.BlockSpec((woa.shape[0], tn), col), pl.BlockSpec((wob.shape[0], tn), col)],
        out_specs=pl.BlockSpec((tm, tn), lambda i, j: (i, j)),
        out_shape=jax.ShapeDtypeStruct((t, d), BF16),
        compiler_params=_params("arbitrary", "arbitrary"),
        name="gated_merge",
    )(h, oa, ob, wg, wg, woa, wob)


def _outproj_kernel(m_ref, w_ref, x_ref, mod_ref, n2_ref, x1_ref, h2_ref):
    x1 = x_ref[...] + mod_ref[2:3, :] * _dot(m_ref[...], w_ref[...])
    x1_ref[...] = x1
    h2_ref[...] = (_rms(x1, n2_ref[...]) * (1.0 + mod_ref[4:5, :]) + mod_ref[3:4, :]).astype(BF16)


def _outproj(merged, wout, xf, mod3, n2, seq, tm):
    t, d = xf.shape
    per_b = seq // tm
    row = lambda i: (i, 0)
    return pl.pallas_call(
        _outproj_kernel,
        grid=(t // tm,),
        in_specs=[pl.BlockSpec((tm, d), row), pl.BlockSpec((d, d), lambda i: (0, 0)),
                  pl.BlockSpec((tm, d), row),
                  pl.BlockSpec((None, 6, d), lambda i: (i // per_b, 0, 0)),
                  pl.BlockSpec((1, d), lambda i: (0, 0))],
        out_specs=[pl.BlockSpec((tm, d), row), pl.BlockSpec((tm, d), row)],
        out_shape=[jax.ShapeDtypeStruct((t, d), F32), jax.ShapeDtypeStruct((t, d), BF16)],
        compiler_params=_params("arbitrary"),
        name="outproj_norm2",
    )(merged, wout, xf, mod3, n2)


def _top_values(work, out_ref, want_rank=False):
    out_ref[...] = jnp.full(out_ref.shape, NEG_INF, F32)
    rank = jnp.full(work.shape, float(NRANK), F32) if want_rank else None
    for r in range(NRANK):
        m = jnp.max(work, axis=0, keepdims=True)
        out_ref[r:r + 1, :] = m
        hit = work == m
        if want_rank:
            rank = jnp.where(hit, float(r), rank)
        work = jnp.where(hit, NEG_INF, work)
    return rank


def _peer_route_kernel(h_ref, wqt_ref, keys_ref, rj_ref, ri_ref, va_ref, vb_ref, vc_ref):
    tm = h_ref.shape[0]
    pq_t = lax.dot_general(wqt_ref[...], h_ref[...], NT_DIMS, preferred_element_type=F32)
    rowi = lax.broadcasted_iota(jnp.int32, (SUBLANES, tm), 0)
    for hd in range(PEER_HEADS):
        sub = []
        rank2 = None
        for p, vref in ((0, va_ref), (1, vb_ref)):
            g = 2 * hd + p
            s = _dot(keys_ref[g], pq_t[g * PEER_HALF:(g + 1) * PEER_HALF, :].astype(BF16))
            sub.append(s)
            rank2 = _top_values(s, vref, want_rank=(p == 1))
        s1, s2 = sub
        slabs = [va_ref[0:1, :] + vb_ref[...]]
        for k in range(1, SUBLANES):
            n_l = NRANK // (k + 1)
            sl = va_ref[k:k + 1, :] + vb_ref[0:SUBLANES, :]
            slabs.append(sl if n_l >= SUBLANES else jnp.where(rowi < n_l, sl, NEG_INF))
        slabs.append(va_ref[SUBLANES:, :] + vb_ref[0:1, :])
        _top_values(jnp.concatenate(slabs, axis=0), vc_ref)
        c0 = vc_ref[0:1, :]
        z = jnp.sum(jnp.exp(vc_ref[0:PEER_TOPK, :] - c0), axis=0, keepdims=True)
        thr = 0.5 * (vc_ref[PEER_TOPK - 1:PEER_TOPK, :] + vc_ref[PEER_TOPK:PEER_TOPK + 1, :]) - s1
        count = jnp.zeros(thr.shape, F32)
        for r in range(NRANK):
            count = count + jnp.where(vb_ref[r:r + 1, :] >= thr, 1.0, 0.0)
        rj_ref[hd, 0] = pltpu.bitcast(rank2.astype(BF16), jnp.int32)
        rj_ref[hd, 1] = pltpu.bitcast((jnp.exp(s2 - vb_ref[0:1, :]) * (1.0 / z)).astype(BF16), jnp.int32)
        ri_ref[hd, 0] = count
        ri_ref[hd, 1] = jnp.exp(s1 - va_ref[0:1, :])


def _peer_route(h2, wqt, keys_b, tm):
    t, d = h2.shape
    return pl.pallas_call(
        _peer_route_kernel,
        grid=(t // tm,),
        in_specs=[pl.BlockSpec((tm, d), lambda i: (i, 0)),
                  pl.BlockSpec(wqt.shape, lambda i: (0, 0)),
                  pl.BlockSpec(keys_b.shape, lambda i: (0, 0, 0))],
        out_specs=[pl.BlockSpec((PEER_HEADS, 2, PEER_NKEYS // 2, tm), lambda i: (0, 0, 0, i)),
                   pl.BlockSpec((PEER_HEADS, 2, PEER_NKEYS, tm), lambda i: (0, 0, 0, i))],
        out_shape=[jax.ShapeDtypeStruct((PEER_HEADS, 2, PEER_NKEYS // 2, t), jnp.int32),
                   jax.ShapeDtypeStruct((PEER_HEADS, 2, PEER_NKEYS, t), F32)],
        scratch_shapes=[pltpu.VMEM((RANK_ROWS, tm), F32)] * 3,
        compiler_params=_params("arbitrary"),
        name="peer_route",
    )(h2, wqt, keys_b)


def _peer_dense_kernel(h_ref, u_ref, v_ref, rj_ref, ri_ref, x1_ref, mod_ref, o_ref, w_ref):
    e = pl.program_id(1)
    te, tm = w_ref.shape
    pack = 2 * SUBLANES

    @pl.when(e == 0)
    def _():
        o_ref[...] = jnp.zeros(o_ref.shape, F32)

    a_t = lax.dot_general(u_ref[...], h_ref[...], NT_DIMS, preferred_element_type=F32)
    for ii in range(te // PEER_NKEYS):
        for cb in range(tm // LANES):
            cols = pl.ds(cb * LANES, LANES)
            count = [jnp.broadcast_to(ri_ref[hd, 0, ii:ii + 1, cols], (pack, LANES)).astype(BF16)
                     for hd in range(PEER_HEADS)]
            e1 = [jnp.broadcast_to(ri_ref[hd, 1, ii:ii + 1, cols], (pack, LANES)).astype(BF16)
                  for hd in range(PEER_HEADS)]
            for jb in range(PEER_NKEYS // pack):
                rows = pl.ds(jb * SUBLANES, SUBLANES)
                gate = jnp.zeros((pack, LANES), BF16)
                for hd in range(PEER_HEADS):
                    rank2 = pltpu.bitcast(rj_ref[hd, 0, rows, cols], BF16)
                    e2 = pltpu.bitcast(rj_ref[hd, 1, rows, cols], BF16)
                    gate = gate + jnp.where(rank2 < count[hd], e2, jnp.zeros((), BF16)) * e1[hd]
                a = a_t[ii * PEER_NKEYS + jb * pack:ii * PEER_NKEYS + (jb + 1) * pack,
                        cb * LANES:(cb + 1) * LANES]
                gelu = 0.5 * a * (1.0 + lax.erf(a * (2.0 ** -0.5)))
                w_ref[pl.ds(ii * PEER_NKEYS + jb * pack, pack), cols] = gate * gelu.astype(BF16)
    o_ref[...] += lax.dot_general(w_ref[...], v_ref[...], TN_DIMS, preferred_element_type=F32)

    @pl.when(e == pl.num_programs(1) - 1)
    def _():
        o_ref[...] = x1_ref[...] + mod_ref[5:6, :] * o_ref[...]


def _peer_dense(h2, u_b, v_b, route_j, route_i, x1, mod3, seq, tm, te):
    t, d = h2.shape
    n_e = u_b.shape[0]
    per_b = seq // tm
    row = lambda i, e: (i, 0)
    return pl.pallas_call(
        _peer_dense_kernel,
        grid=(t // tm, n_e // te),
        in_specs=[pl.BlockSpec((tm, d), row),
                  pl.BlockSpec((te, d), lambda i, e: (e, 0)),
                  pl.BlockSpec((te, d), lambda i, e: (e, 0)),
                  pl.BlockSpec((PEER_HEADS, 2, PEER_NKEYS // 2, tm), lambda i, e: (0, 0, 0, i)),
                  pl.BlockSpec((PEER_HEADS, 2, te // PEER_NKEYS, tm), lambda i, e: (0, 0, e, i)),
                  pl.BlockSpec((tm, d), row),
                  pl.BlockSpec((None, 6, d), lambda i, e: (i // per_b, 0, 0))],
        out_specs=pl.BlockSpec((tm, d), row),
        out_shape=jax.ShapeDtypeStruct((t, d), F32),
        scratch_shapes=[pltpu.VMEM((te, tm), BF16)],
        compiler_params=_params("arbitrary", "arbitrary"),
        name="peer_dense",
    )(h2, u_b, v_b, route_j, route_i, x1, mod3)


def _pad_lanes(a, width):
    return jnp.concatenate([a, jnp.zeros(a.shape[:-1] + (width - a.shape[-1],), a.dtype)], axis=-1)


def kernel(x, c, positions, norm1_w, norm2_w, w_ada, b_ada, w_att_in, mla_q_norm, w_mla_qb, mla_kv_norm,
           w_mla_kvb, mla_qk_norm_q, mla_qk_norm_k, w_mla_o, diff_q_norm, diff_k_norm, diff_lambda,
           diff_subln, w_diff_o, w_att_out, w_peer_q, peer_keys, peer_u, peer_v):
    bsz, seq, d = x.shape
    t = bsz * seq
    tm = min(512, seq)
    tq = min(256, seq)
    xf = x.reshape(t, d)
    pos = positions.reshape(t).astype(F32)

    cos_m, sin_m = _rope_tables(pos, MLA_ROPE)
    z64 = jnp.zeros((t, LANES - MLA_ROPE), F32)
    mla_cos = jnp.concatenate([cos_m, cos_m, z64], axis=-1)
    mla_sin = jnp.concatenate([-sin_m, sin_m, z64], axis=-1)
    cos_d, sin_d = _rope_tables(pos, DIFF_ROT)
    half = DIFF_ROT // 2
    rest = DIFF_D - DIFF_ROT
    z8 = jnp.zeros((t, half), F32)
    diff_cos = jnp.tile(jnp.concatenate([cos_d, cos_d, jnp.ones((t, rest), F32)], axis=-1), (1, 2))
    diff_sa = jnp.tile(jnp.concatenate([-sin_d, z8, jnp.zeros((t, rest), F32)], axis=-1), (1, 2))
    diff_sb = jnp.tile(jnp.concatenate([z8, sin_d, jnp.zeros((t, rest), F32)], axis=-1), (1, 2))

    n_lat = MLA_Q_LORA + MLA_KV_LORA + MLA_ROPE
    n_diff = 3 * DIFF_HEADS * DIFF_V
    for l in range(w_ada.shape[0]):
        lambda_init = 0.8 - 0.6 * math.exp(-0.3 * l)
        w_in = w_att_in[l].astype(BF16)
        w1 = _pad_lanes(w_in[:, :n_lat], n_lat + LANES - MLA_ROPE)
        w2 = w_in[:, n_lat:n_lat + n_diff]
        wg = w_in[:, n_lat + n_diff:]
        wqb = w_mla_qb[l].astype(BF16).reshape(MLA_Q_LORA, MLA_HEADS, MLA_QK)
        wqn = wqb[:, :, :MLA_NOPE].reshape(MLA_Q_LORA, MLA_HEADS * MLA_NOPE)
        wqr = _pad_lanes(wqb[:, :, MLA_NOPE:], LANES).reshape(MLA_Q_LORA, MLA_HEADS * LANES)
        wkvb = w_mla_kvb[l].astype(BF16).reshape(MLA_KV_LORA, MLA_HEADS, MLA_NOPE + MLA_V)
        wkn = wkvb[:, :, :MLA_NOPE].reshape(MLA_KV_LORA, MLA_HEADS * MLA_NOPE)
        wv = wkvb[:, :, MLA_NOPE:].reshape(MLA_KV_LORA, MLA_HEADS * MLA_V)
        gq = _pad_lanes(mla_qk_norm_q[l].reshape(1, MLA_QK), 2 * LANES)
        gk = _pad_lanes(mla_qk_norm_k[l].reshape(1, MLA_QK), 2 * LANES)
        g3 = jnp.stack([jnp.tile(diff_q_norm[l], 2) * (DIFF_D ** -0.5 * LOG2E),
                        jnp.tile(diff_k_norm[l], 2),
                        jnp.ones((LANES,), F32)]).reshape(3, 1, LANES)

        mod3 = _modulation(c, w_ada[l], b_ada[l]).reshape(bsz, 6, d)

        h, q, k, v = _mla_prep(xf, mod3, norm1_w[l].reshape(1, d), w1,
                               mla_q_norm[l].reshape(1, -1), mla_kv_norm[l].reshape(1, -1),
                               wqn, wqr, wkn, wv, gq, gk, mla_cos, mla_sin, seq, tm)
        dqkv = _diff_prep(h, w2, g3, diff_cos, diff_sa, diff_sb, tm)
        o_a = _mla_attn(q.reshape(bsz, seq, -1), k.reshape(bsz, seq, -1), v.reshape(bsz, seq, -1), tq)
        o_b = _diff_attn(dqkv.reshape(bsz, seq, -1), diff_lambda[l], diff_subln[l].reshape(1, DIFF_V),
                         tq, lambda_init)
        merged = _merge(h, o_a.reshape(t, -1), o_b.reshape(t, -1), wg, w_mla_o[l].astype(BF16),
                        w_diff_o[l].astype(BF16), tm, 512)
        x1, h2 = _outproj(merged, w_att_out[l].astype(BF16), xf, mod3, norm2_w[l].reshape(1, d), seq, tm)

        route_j, route_i = _peer_route(h2, w_peer_q[l].T.astype(BF16),
                                       peer_keys[l].reshape(2 * PEER_HEADS, PEER_NKEYS, PEER_HALF).astype(BF16),
                                       min(256, seq))
        xf = _peer_dense(h2, peer_u[l].astype(BF16), peer_v[l].astype(BF16), route_j, route_i, x1, mod3,
                         seq, tm, 1024)
    return xf.reshape(bsz, seq, d)
```

```python
import functools
import math
from typing import NamedTuple

import jax
import jax.numpy as jnp
from jax import lax
from jax.experimental import pallas as pl
from jax.experimental.pallas import tpu as pltpu

F32 = jnp.float32
BF16 = jnp.bfloat16

ROPE_THETA = 500000.0
EPS = 1e-6
MLA_HEADS = 8
MLA_Q_LORA = 512
MLA_KV_LORA = 512
MLA_NOPE = 128
MLA_ROPE = 64
MLA_V = 128
MLA_QK = MLA_NOPE + MLA_ROPE
DIFF_HEADS = 8
DIFF_D = 64
DIFF_V = 2 * DIFF_D
DIFF_ROT = DIFF_D // 4
PEER_HEADS = 8
PEER_NKEYS = 128
PEER_HALF = 128
PEER_TOPK = 16

LANES = 128
SUBLANES = 8
VMEM_LIMIT_BYTES = 56 * 1024 * 1024
LOG2E = 1.4426950408889634
NEG_BIG = float(jnp.finfo(jnp.float32).min)
NEG_INF = float("-inf")
NT_DIMS = (((1,), (1,)), ((), ()))
TN_DIMS = (((0,), (0,)), ((), ()))

NRANK = PEER_TOPK + 1
RANK_ROWS = 24


def _params(*sem, flags=None):
    return pltpu.CompilerParams(dimension_semantics=sem, vmem_limit_bytes=VMEM_LIMIT_BYTES, flags=flags)


def _rms(x, w):
    return x * lax.rsqrt(jnp.mean(x * x, axis=-1, keepdims=True) + EPS) * w


def _dot(a, b):
    return jnp.dot(a, b, preferred_element_type=F32)


def _mod_kernel(c_ref, w_ref, b_ref, o_ref):
    c = c_ref[...]
    a = (c * jax.nn.sigmoid(c)).astype(BF16)
    o_ref[...] = _dot(a, w_ref[...].astype(BF16)) + b_ref[...]


def _modulation(c, w, b):
    bsz, d = c.shape
    n = w.shape[1]
    tn = 1536
    return pl.pallas_call(
        _mod_kernel,
        grid=(n // tn,),
        in_specs=[pl.BlockSpec((bsz, d), lambda j: (0, 0)),
                  pl.BlockSpec((d, tn), lambda j: (0, j)),
                  pl.BlockSpec((1, tn), lambda j: (0, j))],
        out_specs=pl.BlockSpec((bsz, tn), lambda j: (0, j)),
        out_shape=jax.ShapeDtypeStruct((bsz, n), F32),
        compiler_params=_params("arbitrary"),
        name="adaln_mod",
    )(c, w, b.reshape(1, n))


def _trig_kernel(p_ref, inv_ref, cos_ref, sin_ref):
    ang = p_ref[...] * inv_ref[...]
    cos_ref[...] = jnp.cos(ang)
    sin_ref[...] = jnp.sin(ang)


def _rope_tables(pos_f32, r):
    t = pos_f32.shape[0]
    nf = r // 2
    inv = ROPE_THETA ** (-jnp.arange(0, r, 2, dtype=F32) / r)
    rep = LANES // nf
    rows = t // rep
    p = jnp.broadcast_to(pos_f32[:, None], (t, nf)).reshape(rows, LANES)
    inv_row = jnp.tile(inv, rep).reshape(1, LANES)
    tr = min(rows, 1024)
    cos, sin = pl.pallas_call(
        _trig_kernel,
        grid=(rows // tr,),
        in_specs=[pl.BlockSpec((tr, LANES), lambda i: (i, 0)),
                  pl.BlockSpec((1, LANES), lambda i: (0, 0))],
        out_specs=[pl.BlockSpec((tr, LANES), lambda i: (i, 0))] * 2,
        out_shape=[jax.ShapeDtypeStruct((rows, LANES), F32)] * 2,
        compiler_params=_params("arbitrary"),
        name="rope_tables",
    )(p, inv_row)
    return cos.reshape(t, nf), sin.reshape(t, nf)


def _mla_prep_kernel(x_ref, mod_ref, n1_ref, w1_ref, qn_ref, kvn_ref, wqn_ref, wqr_ref, wkn_ref,
                     wv_ref, gq_ref, gk_ref, c_ref, s_ref, h_ref, q_ref, k_ref, v_ref, *, qscale):
    x = x_ref[...]
    h = _rms(x, n1_ref[...]) * (1.0 + mod_ref[1:2, :]) + mod_ref[0:1, :]
    hb = h.astype(BF16)
    h_ref[...] = hb
    lat = _dot(hb, w1_ref[...])
    qn = _rms(lat[:, :MLA_Q_LORA], qn_ref[...]).astype(BF16)
    kvn = _rms(lat[:, MLA_Q_LORA:MLA_Q_LORA + MLA_KV_LORA], kvn_ref[...]).astype(BF16)
    kr = lat[:, MLA_Q_LORA + MLA_KV_LORA:]
    q_nope = _dot(qn, wqn_ref[...])
    q_rope = _dot(qn, wqr_ref[...])
    k_nope = _dot(kvn, wkn_ref[...])
    v_ref[...] = _dot(kvn, wv_ref[...]).astype(BF16)
    cos = c_ref[...]
    sin = s_ref[...]
    gq = gq_ref[...]
    gk = gk_ref[...]
    kr_ss = jnp.sum(kr * kr, axis=-1, keepdims=True)

    def rot(r):
        return r * cos + (pltpu.roll(r, LANES - 32, 1) + pltpu.roll(r, 32, 1)) * sin

    kr_rot = rot(kr * gk[:, LANES:])
    for hd in range(MLA_HEADS):
        lo, hi = hd * LANES, (hd + 1) * LANES
        a, r = q_nope[:, lo:hi], q_rope[:, lo:hi]
        ss = jnp.sum(a * a + r * r, axis=-1, keepdims=True)
        inv = lax.rsqrt(ss * (1.0 / MLA_QK) + EPS) * qscale
        q_ref[:, 2 * lo:2 * lo + LANES] = (a * inv * gq[:, :LANES]).astype(BF16)
        q_ref[:, 2 * lo + LANES:2 * hi] = (rot(r * gq[:, LANES:]) * inv).astype(BF16)
        a = k_nope[:, lo:hi]
        ss = jnp.sum(a * a, axis=-1, keepdims=True) + kr_ss
        inv = lax.rsqrt(ss * (1.0 / MLA_QK) + EPS)
        k_ref[:, 2 * lo:2 * lo + LANES] = (a * inv * gk[:, :LANES]).astype(BF16)
        k_ref[:, 2 * lo + LANES:2 * hi] = (kr_rot * inv).astype(BF16)


def _mla_prep(xf, mod3, n1, w1, qnw, kvnw, wqn, wqr, wkn, wv, gq, gk, cos_t, sin_t, seq, tm):
    t, d = xf.shape
    per_b = seq // tm
    const = lambda i: (0, 0)
    row = lambda i: (i, 0)
    full = lambda a: pl.BlockSpec(a.shape, const)
    return pl.pallas_call(
        functools.partial(_mla_prep_kernel, qscale=MLA_QK ** -0.5 * LOG2E),
        grid=(t // tm,),
        in_specs=[pl.BlockSpec((tm, d), row),
                  pl.BlockSpec((None, 6, d), lambda i: (i // per_b, 0, 0)),
                  full(n1), full(w1), full(qnw), full(kvnw), full(wqn), full(wqr), full(wkn), full(wv),
                  full(gq), full(gk),
                  pl.BlockSpec((tm, LANES), row), pl.BlockSpec((tm, LANES), row)],
        out_specs=[pl.BlockSpec((tm, d), row),
                   pl.BlockSpec((tm, MLA_HEADS * 2 * LANES), row),
                   pl.BlockSpec((tm, MLA_HEADS * 2 * LANES), row),
                   pl.BlockSpec((tm, MLA_HEADS * MLA_V), row)],
        out_shape=[jax.ShapeDtypeStruct((t, d), BF16),
                   jax.ShapeDtypeStruct((t, MLA_HEADS * 2 * LANES), BF16),
                   jax.ShapeDtypeStruct((t, MLA_HEADS * 2 * LANES), BF16),
                   jax.ShapeDtypeStruct((t, MLA_HEADS * MLA_V), BF16)],
        compiler_params=_params("arbitrary"),
        name="mla_prep",
    )(xf, mod3, n1, w1, qnw, kvnw, wqn, wqr, wkn, wv, gq, gk, cos_t, sin_t)


def _diff_prep_kernel(h_ref, w_ref, g_ref, c_ref, sa_ref, sb_ref, o_ref):
    j = pl.program_id(1)
    d = _dot(h_ref[...], w_ref[...])

    @pl.when(j == 2)
    def _():
        o_ref[...] = d.astype(BF16)

    @pl.when(j < 2)
    def _():
        g = g_ref[...]
        cos, sa, sb = c_ref[...], sa_ref[...], sb_ref[...]
        lo = lax.broadcasted_iota(jnp.int32, (d.shape[0], LANES), 1) < DIFF_D
        for cb in range(d.shape[1] // LANES):
            blk = d[:, cb * LANES:(cb + 1) * LANES]
            sq = blk * blk
            s_lo = jnp.sum(jnp.where(lo, sq, 0.0), axis=-1, keepdims=True)
            s_hi = jnp.sum(jnp.where(lo, 0.0, sq), axis=-1, keepdims=True)
            inv = jnp.where(lo, lax.rsqrt(s_lo * (1.0 / DIFF_D) + EPS),
                            lax.rsqrt(s_hi * (1.0 / DIFF_D) + EPS))
            y = blk * inv * g
            y = y * cos + pltpu.roll(y, LANES - DIFF_ROT // 2, 1) * sa + pltpu.roll(y, DIFF_ROT // 2, 1) * sb
            o_ref[:, cb * LANES:(cb + 1) * LANES] = y.astype(BF16)


def _diff_prep(h, w2, g3, cos_t, sa_t, sb_t, tm):
    t, d = h.shape
    n = w2.shape[1]
    tn = n // 3
    row = lambda i, j: (i, 0)
    return pl.pallas_call(
        _diff_prep_kernel,
        grid=(t // tm, 3),
        in_specs=[pl.BlockSpec((tm, d), row),
                  pl.BlockSpec((d, tn), lambda i, j: (0, j)),
                  pl.BlockSpec((None, 1, LANES), lambda i, j: (j, 0, 0)),
                  pl.BlockSpec((tm, LANES), row), pl.BlockSpec((tm, LANES), row),
                  pl.BlockSpec((tm, LANES), row)],
        out_specs=pl.BlockSpec((tm, tn), lambda i, j: (i, j)),
        out_shape=jax.ShapeDtypeStruct((t, n), BF16),
        compiler_params=_params("arbitrary", "arbitrary"),
        name="diff_prep",
    )(h, w2, g3, cos_t, sa_t, sb_t)


def _causal_probs(q, k_ref, i, tq):
    sd = lax.dot_general(q, k_ref[i * tq:(i + 1) * tq, :], NT_DIMS, preferred_element_type=F32)
    rowi = lax.broadcasted_iota(jnp.int32, (tq, tq), 0)
    coli = lax.broadcasted_iota(jnp.int32, (tq, tq), 1)
    sd = jnp.where(coli <= rowi, sd, NEG_BIG)
    m = jnp.max(sd, axis=-1, keepdims=True)
    sp = None
    if i > 0:
        sp = lax.dot_general(q, k_ref[0:i * tq, :], NT_DIMS, preferred_element_type=F32)
        m = jnp.maximum(m, jnp.max(sp, axis=-1, keepdims=True))
    pd = jnp.exp2(sd - m)
    l = jnp.sum(pd, axis=-1, keepdims=True)
    pp = None
    if i > 0:
        pp = jnp.exp2(sp - m)
        l = l + jnp.sum(pp, axis=-1, keepdims=True)
    return pd, pp, l


def _pv(pd, pp, v_ref, i, tq):
    o = _dot(pd.astype(BF16), v_ref[i * tq:(i + 1) * tq, :])
    if pp is not None:
        o = o + _dot(pp.astype(BF16), v_ref[0:i * tq, :])
    return o


def _mla_attn_kernel(q_ref, k_ref, v_ref, o_ref, *, tq):
    for i in range(q_ref.shape[0] // tq):
        pd, pp, l = _causal_probs(q_ref[i * tq:(i + 1) * tq, :], k_ref, i, tq)
        o = _pv(pd, pp, v_ref, i, tq)
        o_ref[i * tq:(i + 1) * tq, :] = (o * (1.0 / l)).astype(BF16)


def _mla_attn(q, k, v, tq):
    b, s, _ = q.shape
    qk = lambda bi, h: (bi, 0, h)
    return pl.pallas_call(
        functools.partial(_mla_attn_kernel, tq=tq),
        grid=(b, MLA_HEADS),
        in_specs=[pl.BlockSpec((None, s, 2 * LANES), qk), pl.BlockSpec((None, s, 2 * LANES), qk),
                  pl.BlockSpec((None, s, MLA_V), qk)],
        out_specs=pl.BlockSpec((None, s, MLA_V), qk),
        out_shape=jax.ShapeDtypeStruct((b, s, MLA_HEADS * MLA_V), BF16),
        compiler_params=_params("arbitrary", "arbitrary"),
        name="mla_attn",
    )(q, k, v)


def _diff_attn_kernel(lam_ref, g_ref, q_ref, k_ref, v_ref, o_ref, *, tq, lambda_init):
    lp = lam_ref[...]
    lam = (jnp.exp(jnp.sum(lp[0:1] * lp[1:2], axis=-1, keepdims=True))
           - jnp.exp(jnp.sum(lp[2:3] * lp[3:4], axis=-1, keepdims=True)) + lambda_init)
    g = g_ref[...] * (1.0 - lambda_init)
    lo = lax.broadcasted_iota(jnp.int32, (tq, LANES), 1) < DIFF_D
    for i in range(q_ref.shape[0] // tq):
        q = q_ref[i * tq:(i + 1) * tq, :]
        zero = jnp.zeros_like(q)
        pd1, pp1, l1 = _causal_probs(jnp.where(lo, q, zero), k_ref, i, tq)
        pd2, pp2, l2 = _causal_probs(jnp.where(lo, zero, q), k_ref, i, tq)
        c1 = 1.0 / l1
        c2 = lam / l2
        wd = pd1 * c1 - pd2 * c2
        wp = None if pp1 is None else pp1 * c1 - pp2 * c2
        o = _pv(wd, wp, v_ref, i, tq)
        o_ref[i * tq:(i + 1) * tq, :] = _rms(o, g).astype(BF16)


def _diff_attn(qkv, lam_p, subln, tq, lambda_init):
    b, s, _ = qkv.shape
    return pl.pallas_call(
        functools.partial(_diff_attn_kernel, tq=tq, lambda_init=lambda_init),
        grid=(b, DIFF_HEADS),
        in_specs=[pl.BlockSpec(lam_p.shape, lambda bi, h: (0, 0)),
                  pl.BlockSpec(subln.shape, lambda bi, h: (0, 0)),
                  pl.BlockSpec((None, s, LANES), lambda bi, h: (bi, 0, h)),
                  pl.BlockSpec((None, s, LANES), lambda bi, h: (bi, 0, DIFF_HEADS + h)),
                  pl.BlockSpec((None, s, LANES), lambda bi, h: (bi, 0, 2 * DIFF_HEADS + h))],
        out_specs=pl.BlockSpec((None, s, DIFF_V), lambda bi, h: (bi, 0, h)),
        out_shape=jax.ShapeDtypeStruct((b, s, DIFF_HEADS * DIFF_V), BF16),
        compiler_params=_params("arbitrary", "arbitrary"),
        name="diff_attn",
    )(lam_p, subln, qkv, qkv, qkv)


def _merge_kernel(h_ref, oa_ref, ob_ref, wga_ref, wgb_ref, woa_ref, wob_ref, o_ref):
    h = h_ref[...]
    ya = _dot(oa_ref[...], woa_ref[...])
    yb = _dot(ob_ref[...], wob_ref[...])
    ga = jax.nn.sigmoid(_dot(h, wga_ref[...]))
    gb = jax.nn.sigmoid(_dot(h, wgb_ref[...]))
    o_ref[...] = (ga * ya + gb * yb).astype(BF16)


def _merge(h, oa, ob, wg, woa, wob, tm, tn):
    t, d = h.shape
    nj = d // tn
    row = lambda i, j: (i, 0)
    col = lambda i, j: (0, j)
    return pl.pallas_call(
        _merge_kernel,
        grid=(t // tm, nj),
        in_specs=[pl.BlockSpec((tm, d), row),
                  pl.BlockSpec((tm, oa.shape[1]), row), pl.BlockSpec((tm, ob.shape[1]), row),
                  pl.BlockSpec((d, tn), col), pl.BlockSpec((d, tn), lambda i, j: (0, nj + j)),
                  pl.BlockSpec((woa.shape[0], tn), col), pl.BlockSpec((wob.shape[0], tn), col)],
        out_specs=pl.BlockSpec((tm, tn), lambda i, j: (i, j)),
        out_shape=jax.ShapeDtypeStruct((t, d), BF16),
        compiler_params=_params("arbitrary", "arbitrary"),
        name="gated_merge",
    )(h, oa, ob, wg, wg, woa, wob)


def _outproj_kernel(m_ref, w_ref, x_ref, mod_ref, n2_ref, x1_ref, h2_ref):
    x1 = x_ref[...] + mod_ref[2:3, :] * _dot(m_ref[...], w_ref[...])
    x1_ref[...] = x1
    h2_ref[...] = (_rms(x1, n2_ref[...]) * (1.0 + mod_ref[4:5, :]) + mod_ref[3:4, :]).astype(BF16)


def _outproj(merged, wout, xf, mod3, n2, seq, tm):
    t, d = xf.shape
    per_b = seq // tm
    row = lambda i: (i, 0)
    return pl.pallas_call(
        _outproj_kernel,
        grid=(t // tm,),
        in_specs=[pl.BlockSpec((tm, d), row), pl.BlockSpec((d, d), lambda i: (0, 0)),
                  pl.BlockSpec((tm, d), row),
                  pl.BlockSpec((None, 6, d), lambda i: (i // per_b, 0, 0)),
                  pl.BlockSpec((1, d), lambda i: (0, 0))],
        out_specs=[pl.BlockSpec((tm, d), row), pl.BlockSpec((tm, d), row)],
        out_shape=[jax.ShapeDtypeStruct((t, d), F32), jax.ShapeDtypeStruct((t, d), BF16)],
        compiler_params=_params("arbitrary"),
        name="outproj_norm2",
    )(merged, wout, xf, mod3, n2)


def _sort16_network():
    def merge(lo, hi, r):
        step = 2 * r
        if step < hi - lo:
            yield from merge(lo, hi, step)
            yield from merge(lo + r, hi, step)
            yield from [(i, i + r) for i in range(lo + r, hi - r, step)]
        else:
            yield (lo, lo + r)

    def sort(lo, hi):
        if hi > lo:
            mid = lo + (hi - lo) // 2
            yield from sort(lo, mid)
            yield from sort(mid + 1, hi)
            yield from merge(lo, hi, 1)

    return tuple(sort(0, PEER_TOPK - 1))


SORT16 = _sort16_network()


def _compare_exchange(x, i, j):
    x[i], x[j] = jnp.maximum(x[i], x[j]), jnp.minimum(x[i], x[j])


def _ranked_values(s, out_ref, want_rank):
    n = PEER_TOPK
    groups = [s[SUBLANES * r:SUBLANES * (r + 1), :] for r in range(s.shape[0] // SUBLANES)]
    x = list(groups)
    for i, j in SORT16:
        _compare_exchange(x, i, j)
    for shift in (4, 2, 1):
        x = [jnp.maximum(x[r], pltpu.roll(x[n - 1 - r], shift, 0)) for r in range(n)]
        dist = n // 2
        while dist >= 1:
            for i in range(n):
                if not i & dist:
                    _compare_exchange(x, i, i + dist)
            dist //= 2
    below = [jnp.where(g < x[n - 1], g, NEG_INF) for g in groups]
    while len(below) > 1:
        below = [jnp.maximum(below[2 * i], below[2 * i + 1]) for i in range(len(below) // 2)]
    nxt = jnp.max(below[0], axis=0, keepdims=True)
    out_ref[...] = jnp.full(out_ref.shape, NEG_INF, F32)
    for r in range(n):
        out_ref[r:r + 1, :] = x[r][0:1, :]
    out_ref[n:n + 1, :] = nxt
    if not want_rank:
        return None
    ranks = []
    for g in groups:
        rank = jnp.where(g >= nxt, float(n), float(NRANK))
        for r in range(n - 1, -1, -1):
            rank = jnp.where(g >= x[r], float(r), rank)
        ranks.append(rank)
    return jnp.concatenate(ranks, axis=0)


def _peer_route_kernel(h_ref, wqt_ref, keys_ref, rj_ref, ri_ref, va_ref, vb_ref, vc_ref):
    tm = h_ref.shape[0]
    pq_t = lax.dot_general(wqt_ref[...], h_ref[...], NT_DIMS, preferred_element_type=F32)
    rowi = lax.broadcasted_iota(jnp.int32, (SUBLANES, tm), 0)
    for hd in range(PEER_HEADS):
        sub = []
        rank2 = None
        for p, vref in ((0, va_ref), (1, vb_ref)):
            g = 2 * hd + p
            s = _dot(keys_ref[g], pq_t[g * PEER_HALF:(g + 1) * PEER_HALF, :].astype(BF16))
            sub.append(s)
            tiles = [_ranked_values(s[:, cb * LANES:(cb + 1) * LANES], vref.at[:, pl.ds(cb * LANES, LANES)],
                                    want_rank=(p == 1)) for cb in range(tm // LANES)]
            rank2 = jnp.concatenate(tiles, axis=1) if p == 1 else None
        s1, s2 = sub
        slabs = [va_ref[0:1, :] + vb_ref[...]]
        for k in range(1, SUBLANES):
            n_l = NRANK // (k + 1)
            sl = va_ref[k:k + 1, :] + vb_ref[0:SUBLANES, :]
            slabs.append(sl if n_l >= SUBLANES else jnp.where(rowi < n_l, sl, NEG_INF))
        slabs.append(va_ref[SUBLANES:, :] + vb_ref[0:1, :])
        n_cand = sum(sl.shape[0] for sl in slabs)
        slabs.append(jnp.full((PEER_NKEYS - n_cand, tm), NEG_INF, F32))
        cand = jnp.concatenate(slabs, axis=0)
        for cb in range(tm // LANES):
            _ranked_values(cand[:, cb * LANES:(cb + 1) * LANES], vc_ref.at[:, pl.ds(cb * LANES, LANES)], False)
        c0 = vc_ref[0:1, :]
        z = jnp.sum(jnp.exp(vc_ref[0:PEER_TOPK, :] - c0), axis=0, keepdims=True)
        thr = 0.5 * (vc_ref[PEER_TOPK - 1:PEER_TOPK, :] + vc_ref[PEER_TOPK:PEER_TOPK + 1, :]) - s1
        count = jnp.zeros(thr.shape, F32)
        for r in range(NRANK):
            count = jnp.where(vb_ref[r:r + 1, :] >= thr, float(r + 1), count)
        rj_ref[hd, 0] = pltpu.bitcast(rank2.astype(BF16), jnp.int32)
        rj_ref[hd, 1] = pltpu.bitcast((jnp.exp(s2 - vb_ref[0:1, :]) * (1.0 / z)).astype(BF16), jnp.int32)
        ri_ref[hd, 0] = count
        ri_ref[hd, 1] = 0.5 * jnp.exp(s1 - va_ref[0:1, :])


def _peer_route(h2, wqt, keys_b, tm):
    t, d = h2.shape
    return pl.pallas_call(
        _peer_route_kernel,
        grid=(t // tm,),
        in_specs=[pl.BlockSpec((tm, d), lambda i: (i, 0)),
                  pl.BlockSpec(wqt.shape, lambda i: (0, 0)),
                  pl.BlockSpec(keys_b.shape, lambda i: (0, 0, 0))],
        out_specs=[pl.BlockSpec((PEER_HEADS, 2, PEER_NKEYS // 2, tm), lambda i: (0, 0, 0, i)),
                   pl.BlockSpec((PEER_HEADS, 2, PEER_NKEYS, tm), lambda i: (0, 0, 0, i))],
        out_shape=[jax.ShapeDtypeStruct((PEER_HEADS, 2, PEER_NKEYS // 2, t), jnp.int32),
                   jax.ShapeDtypeStruct((PEER_HEADS, 2, PEER_NKEYS, t), F32)],
        scratch_shapes=[pltpu.VMEM((RANK_ROWS, tm), F32)] * 3,
        compiler_params=_params("arbitrary"),
        name="peer_route",
    )(h2, wqt, keys_b)


def _peer_dense_kernel(h_ref, u_ref, v_ref, rj_ref, ri_ref, x1_ref, mod_ref, o_ref, w_ref):
    e = pl.program_id(1)
    te, tm = w_ref.shape
    pack = 2 * SUBLANES

    @pl.when(e == 0)
    def _():
        o_ref[...] = jnp.zeros(o_ref.shape, F32)

    a_t = lax.dot_general(u_ref[...], h_ref[...], NT_DIMS, preferred_element_type=F32)
    for ii in range(te // PEER_NKEYS):
        for cb in range(tm // LANES):
            cols = pl.ds(cb * LANES, LANES)
            count = [jnp.broadcast_to(ri_ref[hd, 0, ii:ii + 1, cols], (pack, LANES)).astype(BF16)
                     for hd in range(PEER_HEADS)]
            e1 = [jnp.broadcast_to(ri_ref[hd, 1, ii:ii + 1, cols], (pack, LANES)).astype(BF16)
                  for hd in range(PEER_HEADS)]
            for jb in range(PEER_NKEYS // pack):
                rows = pl.ds(jb * SUBLANES, SUBLANES)
                gate = jnp.zeros((pack, LANES), BF16)
                for hd in range(PEER_HEADS):
                    rank2 = pltpu.bitcast(rj_ref[hd, 0, rows, cols], BF16)
                    e2 = pltpu.bitcast(rj_ref[hd, 1, rows, cols], BF16)
                    gate = gate + jnp.where(rank2 < count[hd], e2, jnp.zeros((), BF16)) * e1[hd]
                a = a_t[ii * PEER_NKEYS + jb * pack:ii * PEER_NKEYS + (jb + 1) * pack,
                        cb * LANES:(cb + 1) * LANES]
                gelu2 = a * (1.0 + lax.erf(a * (2.0 ** -0.5)))
                w_ref[pl.ds(ii * PEER_NKEYS + jb * pack, pack), cols] = gate * gelu2.astype(BF16)
    o_ref[...] += lax.dot_general(w_ref[...], v_ref[...], TN_DIMS, preferred_element_type=F32)

    @pl.when(e == pl.num_programs(1) - 1)
    def _():
        o_ref[...] = x1_ref[...] + mod_ref[5:6, :] * o_ref[...]


def _peer_dense(h2, u_b, v_b, route_j, route_i, x1, mod3, seq, tm, te):
    t, d = h2.shape
    n_e = u_b.shape[0]
    per_b = seq // tm
    row = lambda i, e: (i, 0)
    return pl.pallas_call(
        _peer_dense_kernel,
        grid=(t // tm, n_e // te),
        in_specs=[pl.BlockSpec((tm, d), row),
                  pl.BlockSpec((te, d), lambda i, e: (e, 0)),
                  pl.BlockSpec((te, d), lambda i, e: (e, 0)),
                  pl.BlockSpec((PEER_HEADS, 2, PEER_NKEYS // 2, tm), lambda i, e: (0, 0, 0, i)),
                  pl.BlockSpec((PEER_HEADS, 2, te // PEER_NKEYS, tm), lambda i, e: (0, 0, e, i)),
                  pl.BlockSpec((tm, d), row),
                  pl.BlockSpec((None, 6, d), lambda i, e: (i // per_b, 0, 0))],
        out_specs=pl.BlockSpec((tm, d), row),
        out_shape=jax.ShapeDtypeStruct((t, d), F32),
        scratch_shapes=[pltpu.VMEM((te, tm), BF16)],
        compiler_params=_params("arbitrary", "arbitrary"),
        name="peer_dense",
    )(h2, u_b, v_b, route_j, route_i, x1, mod3)


class _Tiles(NamedTuple):
    rows: int
    attn_q: int
    merge_cols: int
    route_tokens: int
    expert_block: int


def _tiles(seq):
    return _Tiles(rows=min(512, seq), attn_q=min(256, seq), merge_cols=512, route_tokens=min(256, seq),
                  expert_block=8 * PEER_NKEYS)


def _pad_lanes(a, width):
    return jnp.concatenate([a, jnp.zeros(a.shape[:-1] + (width - a.shape[-1],), a.dtype)], axis=-1)


def kernel(x, c, positions, norm1_w, norm2_w, w_ada, b_ada, w_att_in, mla_q_norm, w_mla_qb, mla_kv_norm,
           w_mla_kvb, mla_qk_norm_q, mla_qk_norm_k, w_mla_o, diff_q_norm, diff_k_norm, diff_lambda,
           diff_subln, w_diff_o, w_att_out, w_peer_q, peer_keys, peer_u, peer_v):
    bsz, seq, d = x.shape
    t = bsz * seq
    tiles = _tiles(seq)
    tm, tq = tiles.rows, tiles.attn_q
    xf = x.reshape(t, d)
    pos = positions.reshape(t).astype(F32)

    cos_m, sin_m = _rope_tables(pos, MLA_ROPE)
    z64 = jnp.zeros((t, LANES - MLA_ROPE), F32)
    mla_cos = jnp.concatenate([cos_m, cos_m, z64], axis=-1)
    mla_sin = jnp.concatenate([-sin_m, sin_m, z64], axis=-1)
    cos_d, sin_d = _rope_tables(pos, DIFF_ROT)
    half = DIFF_ROT // 2
    rest = DIFF_D - DIFF_ROT
    z8 = jnp.zeros((t, half), F32)
    diff_cos = jnp.tile(jnp.concatenate([cos_d, cos_d, jnp.ones((t, rest), F32)], axis=-1), (1, 2))
    diff_sa = jnp.tile(jnp.concatenate([-sin_d, z8, jnp.zeros((t, rest), F32)], axis=-1), (1, 2))
    diff_sb = jnp.tile(jnp.concatenate([z8, sin_d, jnp.zeros((t, rest), F32)], axis=-1), (1, 2))

    n_lat = MLA_Q_LORA + MLA_KV_LORA + MLA_ROPE
    n_diff = 3 * DIFF_HEADS * DIFF_V
    for l in range(w_ada.shape[0]):
        lambda_init = 0.8 - 0.6 * math.exp(-0.3 * l)
        w_in = w_att_in[l].astype(BF16)
        w1 = _pad_lanes(w_in[:, :n_lat], n_lat + LANES - MLA_ROPE)
        w2 = w_in[:, n_lat:n_lat + n_diff]
        wg = w_in[:, n_lat + n_diff:]
        wqb = w_mla_qb[l].astype(BF16).reshape(MLA_Q_LORA, MLA_HEADS, MLA_QK)
        wqn = wqb[:, :, :MLA_NOPE].reshape(MLA_Q_LORA, MLA_HEADS * MLA_NOPE)
        wqr = _pad_lanes(wqb[:, :, MLA_NOPE:], LANES).reshape(MLA_Q_LORA, MLA_HEADS * LANES)
        wkvb = w_mla_kvb[l].astype(BF16).reshape(MLA_KV_LORA, MLA_HEADS, MLA_NOPE + MLA_V)
        wkn = wkvb[:, :, :MLA_NOPE].reshape(MLA_KV_LORA, MLA_HEADS * MLA_NOPE)
        wv = wkvb[:, :, MLA_NOPE:].reshape(MLA_KV_LORA, MLA_HEADS * MLA_V)
        gq = _pad_lanes(mla_qk_norm_q[l].reshape(1, MLA_QK), 2 * LANES)
        gk = _pad_lanes(mla_qk_norm_k[l].reshape(1, MLA_QK), 2 * LANES)
        g3 = jnp.stack([jnp.tile(diff_q_norm[l], 2) * (DIFF_D ** -0.5 * LOG2E),
                        jnp.tile(diff_k_norm[l], 2),
                        jnp.ones((LANES,), F32)]).reshape(3, 1, LANES)

        mod3 = _modulation(c, w_ada[l], b_ada[l]).reshape(bsz, 6, d)

        h, q, k, v = _mla_prep(xf, mod3, norm1_w[l].reshape(1, d), w1,
                               mla_q_norm[l].reshape(1, -1), mla_kv_norm[l].reshape(1, -1),
                               wqn, wqr, wkn, wv, gq, gk, mla_cos, mla_sin, seq, tm)
        dqkv = _diff_prep(h, w2, g3, diff_cos, diff_sa, diff_sb, tm)
        o_a = _mla_attn(q.reshape(bsz, seq, -1), k.reshape(bsz, seq, -1), v.reshape(bsz, seq, -1), tq)
        o_b = _diff_attn(dqkv.reshape(bsz, seq, -1), diff_lambda[l], diff_subln[l].reshape(1, DIFF_V),
                         tq, lambda_init)
        merged = _merge(h, o_a.reshape(t, -1), o_b.reshape(t, -1), wg, w_mla_o[l].astype(BF16),
                        w_diff_o[l].astype(BF16), tm, tiles.merge_cols)
        x1, h2 = _outproj(merged, w_att_out[l].astype(BF16), xf, mod3, norm2_w[l].reshape(1, d), seq, tm)

        route_j, route_i = _peer_route(h2, w_peer_q[l].T.astype(BF16),
                                       peer_keys[l].reshape(2 * PEER_HEADS, PEER_NKEYS, PEER_HALF).astype(BF16),
                                       tiles.route_tokens)
        xf = _peer_dense(h2, peer_u[l].astype(BF16), peer_v[l].astype(BF16), route_j, route_i, x1, mod3,
                         seq, tm, tiles.expert_block)
    return xf.reshape(bsz, seq, d)
```

```python
import functools
import math
from typing import NamedTuple

import jax
import jax.numpy as jnp
from jax import lax
from jax.experimental import pallas as pl
from jax.experimental.pallas import tpu as pltpu

F32 = jnp.float32
BF16 = jnp.bfloat16

ROPE_THETA = 500000.0
EPS = 1e-6
MLA_HEADS = 8
MLA_Q_LORA = 512
MLA_KV_LORA = 512
MLA_NOPE = 128
MLA_ROPE = 64
MLA_V = 128
MLA_QK = MLA_NOPE + MLA_ROPE
DIFF_HEADS = 8
DIFF_D = 64
DIFF_V = 2 * DIFF_D
DIFF_ROT = DIFF_D // 4
PEER_HEADS = 8
PEER_NKEYS = 128
PEER_HALF = 128
PEER_TOPK = 16

LANES = 128
SUBLANES = 8
VMEM_LIMIT_BYTES = 56 * 1024 * 1024
LOG2E = 1.4426950408889634
NEG_BIG = float(jnp.finfo(jnp.float32).min)
NEG_INF = float("-inf")
NT_DIMS = (((1,), (1,)), ((), ()))
TN_DIMS = (((0,), (0,)), ((), ()))

NRANK = PEER_TOPK + 1
RANK_ROWS = 24


def _params(*sem, flags=None):
    return pltpu.CompilerParams(dimension_semantics=sem, vmem_limit_bytes=VMEM_LIMIT_BYTES, flags=flags)


def _rms(x, w):
    return x * lax.rsqrt(jnp.mean(x * x, axis=-1, keepdims=True) + EPS) * w


def _dot(a, b):
    return jnp.dot(a, b, preferred_element_type=F32)


def _mod_kernel(c_ref, w_ref, b_ref, o_ref):
    c = c_ref[...]
    a = (c * jax.nn.sigmoid(c)).astype(BF16)
    o_ref[...] = _dot(a, w_ref[...].astype(BF16)) + b_ref[...]


def _modulation(c, w, b):
    bsz, d = c.shape
    n = w.shape[1]
    tn = 1536
    return pl.pallas_call(
        _mod_kernel,
        grid=(n // tn,),
        in_specs=[pl.BlockSpec((bsz, d), lambda j: (0, 0)),
                  pl.BlockSpec((d, tn), lambda j: (0, j)),
                  pl.BlockSpec((1, tn), lambda j: (0, j))],
        out_specs=pl.BlockSpec((bsz, tn), lambda j: (0, j)),
        out_shape=jax.ShapeDtypeStruct((bsz, n), F32),
        compiler_params=_params("arbitrary"),
        name="adaln_mod",
    )(c, w, b.reshape(1, n))


def _trig_kernel(p_ref, inv_ref, cos_ref, sin_ref):
    ang = p_ref[...] * inv_ref[...]
    cos_ref[...] = jnp.cos(ang)
    sin_ref[...] = jnp.sin(ang)


def _rope_tables(pos_f32, r):
    t = pos_f32.shape[0]
    nf = r // 2
    inv = ROPE_THETA ** (-jnp.arange(0, r, 2, dtype=F32) / r)
    rep = LANES // nf
    rows = t // rep
    p = jnp.broadcast_to(pos_f32[:, None], (t, nf)).reshape(rows, LANES)
    inv_row = jnp.tile(inv, rep).reshape(1, LANES)
    tr = min(rows, 1024)
    cos, sin = pl.pallas_call(
        _trig_kernel,
        grid=(rows // tr,),
        in_specs=[pl.BlockSpec((tr, LANES), lambda i: (i, 0)),
                  pl.BlockSpec((1, LANES), lambda i: (0, 0))],
        out_specs=[pl.BlockSpec((tr, LANES), lambda i: (i, 0))] * 2,
        out_shape=[jax.ShapeDtypeStruct((rows, LANES), F32)] * 2,
        compiler_params=_params("arbitrary"),
        name="rope_tables",
    )(p, inv_row)
    return cos.reshape(t, nf), sin.reshape(t, nf)


def _mla_prep_kernel(x_ref, mod_ref, n1_ref, w1_ref, qn_ref, kvn_ref, wqn_ref, wqr_ref, wkn_ref,
                     wv_ref, gq_ref, gk_ref, c_ref, s_ref, h_ref, q_ref, k_ref, v_ref, *, qscale):
    x = x_ref[...]
    h = _rms(x, n1_ref[...]) * (1.0 + mod_ref[1:2, :]) + mod_ref[0:1, :]
    hb = h.astype(BF16)
    h_ref[...] = hb
    lat = _dot(hb, w1_ref[...])
    qn = _rms(lat[:, :MLA_Q_LORA], qn_ref[...]).astype(BF16)
    kvn = _rms(lat[:, MLA_Q_LORA:MLA_Q_LORA + MLA_KV_LORA], kvn_ref[...]).astype(BF16)
    kr = lat[:, MLA_Q_LORA + MLA_KV_LORA:]
    q_nope = _dot(qn, wqn_ref[...])
    q_rope = _dot(qn, wqr_ref[...])
    k_nope = _dot(kvn, wkn_ref[...])
    v_ref[...] = _dot(kvn, wv_ref[...]).astype(BF16)
    cos = c_ref[...]
    sin = s_ref[...]
    gq = gq_ref[...]
    gk = gk_ref[...]
    kr_ss = jnp.sum(kr * kr, axis=-1, keepdims=True)

    def rot(r):
        return r * cos + (pltpu.roll(r, LANES - 32, 1) + pltpu.roll(r, 32, 1)) * sin

    kr_rot = rot(kr * gk[:, LANES:])
    for hd in range(MLA_HEADS):
        lo, hi = hd * LANES, (hd + 1) * LANES
        a, r = q_nope[:, lo:hi], q_rope[:, lo:hi]
        ss = jnp.sum(a * a + r * r, axis=-1, keepdims=True)
        inv = lax.rsqrt(ss * (1.0 / MLA_QK) + EPS) * qscale
        q_ref[:, 2 * lo:2 * lo + LANES] = (a * inv * gq[:, :LANES]).astype(BF16)
        q_ref[:, 2 * lo + LANES:2 * hi] = (rot(r * gq[:, LANES:]) * inv).astype(BF16)
        a = k_nope[:, lo:hi]
        ss = jnp.sum(a * a, axis=-1, keepdims=True) + kr_ss
        inv = lax.rsqrt(ss * (1.0 / MLA_QK) + EPS)
        k_ref[:, 2 * lo:2 * lo + LANES] = (a * inv * gk[:, :LANES]).astype(BF16)
        k_ref[:, 2 * lo + LANES:2 * hi] = (kr_rot * inv).astype(BF16)


def _mla_prep(xf, mod3, n1, w1, qnw, kvnw, wqn, wqr, wkn, wv, gq, gk, cos_t, sin_t, seq, tm):
    t, d = xf.shape
    per_b = seq // tm
    const = lambda i: (0, 0)
    row = lambda i: (i, 0)
    full = lambda a: pl.BlockSpec(a.shape, const)
    return pl.pallas_call(
        functools.partial(_mla_prep_kernel, qscale=MLA_QK ** -0.5 * LOG2E),
        grid=(t // tm,),
        in_specs=[pl.BlockSpec((tm, d), row),
                  pl.BlockSpec((None, 6, d), lambda i: (i // per_b, 0, 0)),
                  full(n1), full(w1), full(qnw), full(kvnw), full(wqn), full(wqr), full(wkn), full(wv),
                  full(gq), full(gk),
                  pl.BlockSpec((tm, LANES), row), pl.BlockSpec((tm, LANES), row)],
        out_specs=[pl.BlockSpec((tm, d), row),
                   pl.BlockSpec((tm, MLA_HEADS * 2 * LANES), row),
                   pl.BlockSpec((tm, MLA_HEADS * 2 * LANES), row),
                   pl.BlockSpec((tm, MLA_HEADS * MLA_V), row)],
        out_shape=[jax.ShapeDtypeStruct((t, d), BF16),
                   jax.ShapeDtypeStruct((t, MLA_HEADS * 2 * LANES), BF16),
                   jax.ShapeDtypeStruct((t, MLA_HEADS * 2 * LANES), BF16),
                   jax.ShapeDtypeStruct((t, MLA_HEADS * MLA_V), BF16)],
        compiler_params=_params("arbitrary"),
        name="mla_prep",
    )(xf, mod3, n1, w1, qnw, kvnw, wqn, wqr, wkn, wv, gq, gk, cos_t, sin_t)


def _diff_prep_kernel(h_ref, w_ref, g_ref, c_ref, sa_ref, sb_ref, o_ref):
    j = pl.program_id(1)
    d = _dot(h_ref[...], w_ref[...])

    @pl.when(j == 2)
    def _():
        o_ref[...] = d.astype(BF16)

    @pl.when(j < 2)
    def _():
        g = g_ref[...]
        cos, sa, sb = c_ref[...], sa_ref[...], sb_ref[...]
        lo = lax.broadcasted_iota(jnp.int32, (d.shape[0], LANES), 1) < DIFF_D
        for cb in range(d.shape[1] // LANES):
            blk = d[:, cb * LANES:(cb + 1) * LANES]
            sq = blk * blk
            s_lo = jnp.sum(jnp.where(lo, sq, 0.0), axis=-1, keepdims=True)
            s_hi = jnp.sum(jnp.where(lo, 0.0, sq), axis=-1, keepdims=True)
            inv = jnp.where(lo, lax.rsqrt(s_lo * (1.0 / DIFF_D) + EPS),
                            lax.rsqrt(s_hi * (1.0 / DIFF_D) + EPS))
            y = blk * inv * g
            y = y * cos + pltpu.roll(y, LANES - DIFF_ROT // 2, 1) * sa + pltpu.roll(y, DIFF_ROT // 2, 1) * sb
            o_ref[:, cb * LANES:(cb + 1) * LANES] = y.astype(BF16)


def _diff_prep(h, w2, g3, cos_t, sa_t, sb_t, tm):
    t, d = h.shape
    n = w2.shape[1]
    tn = n // 3
    row = lambda i, j: (i, 0)
    return pl.pallas_call(
        _diff_prep_kernel,
        grid=(t // tm, 3),
        in_specs=[pl.BlockSpec((tm, d), row),
                  pl.BlockSpec((d, tn), lambda i, j: (0, j)),
                  pl.BlockSpec((None, 1, LANES), lambda i, j: (j, 0, 0)),
                  pl.BlockSpec((tm, LANES), row), pl.BlockSpec((tm, LANES), row),
                  pl.BlockSpec((tm, LANES), row)],
        out_specs=pl.BlockSpec((tm, tn), lambda i, j: (i, j)),
        out_shape=jax.ShapeDtypeStruct((t, n), BF16),
        compiler_params=_params("arbitrary", "arbitrary"),
        name="diff_prep",
    )(h, w2, g3, cos_t, sa_t, sb_t)


def _causal_probs(q, k_ref, i, tq):
    sd = lax.dot_general(q, k_ref[i * tq:(i + 1) * tq, :], NT_DIMS, preferred_element_type=F32)
    rowi = lax.broadcasted_iota(jnp.int32, (tq, tq), 0)
    coli = lax.broadcasted_iota(jnp.int32, (tq, tq), 1)
    sd = jnp.where(coli <= rowi, sd, NEG_BIG)
    m = jnp.max(sd, axis=-1, keepdims=True)
    sp = None
    if i > 0:
        sp = lax.dot_general(q, k_ref[0:i * tq, :], NT_DIMS, preferred_element_type=F32)
        m = jnp.maximum(m, jnp.max(sp, axis=-1, keepdims=True))
    pd = jnp.exp2(sd - m)
    l = jnp.sum(pd, axis=-1, keepdims=True)
    pp = None
    if i > 0:
        pp = jnp.exp2(sp - m)
        l = l + jnp.sum(pp, axis=-1, keepdims=True)
    return pd, pp, l


def _pv(pd, pp, v_ref, i, tq):
    o = _dot(pd.astype(BF16), v_ref[i * tq:(i + 1) * tq, :])
    if pp is not None:
        o = o + _dot(pp.astype(BF16), v_ref[0:i * tq, :])
    return o


def _mla_attn_kernel(q_ref, k_ref, v_ref, o_ref, *, tq):
    for i in range(q_ref.shape[0] // tq):
        pd, pp, l = _causal_probs(q_ref[i * tq:(i + 1) * tq, :], k_ref, i, tq)
        o = _pv(pd, pp, v_ref, i, tq)
        o_ref[i * tq:(i + 1) * tq, :] = (o * (1.0 / l)).astype(BF16)


def _mla_attn(q, k, v, tq):
    b, s, _ = q.shape
    qk = lambda bi, h: (bi, 0, h)
    return pl.pallas_call(
        functools.partial(_mla_attn_kernel, tq=tq),
        grid=(b, MLA_HEADS),
        in_specs=[pl.BlockSpec((None, s, 2 * LANES), qk), pl.BlockSpec((None, s, 2 * LANES), qk),
                  pl.BlockSpec((None, s, MLA_V), qk)],
        out_specs=pl.BlockSpec((None, s, MLA_V), qk),
        out_shape=jax.ShapeDtypeStruct((b, s, MLA_HEADS * MLA_V), BF16),
        compiler_params=_params("arbitrary", "arbitrary"),
        name="mla_attn",
    )(q, k, v)


def _diff_attn_kernel(lam_ref, g_ref, q_ref, k_ref, v_ref, o_ref, *, tq, lambda_init):
    lp = lam_ref[...]
    lam = (jnp.exp(jnp.sum(lp[0:1] * lp[1:2], axis=-1, keepdims=True))
           - jnp.exp(jnp.sum(lp[2:3] * lp[3:4], axis=-1, keepdims=True)) + lambda_init)
    g = g_ref[...] * (1.0 - lambda_init)
    lo = lax.broadcasted_iota(jnp.int32, (tq, LANES), 1) < DIFF_D
    for i in range(q_ref.shape[0] // tq):
        q = q_ref[i * tq:(i + 1) * tq, :]
        zero = jnp.zeros_like(q)
        pd1, pp1, l1 = _causal_probs(jnp.where(lo, q, zero), k_ref, i, tq)
        pd2, pp2, l2 = _causal_probs(jnp.where(lo, zero, q), k_ref, i, tq)
        o = _pv(pd1, pp1, v_ref, i, tq) * (1.0 / l1) - _pv(pd2, pp2, v_ref, i, tq) * (lam / l2)
        o_ref[i * tq:(i + 1) * tq, :] = _rms(o, g).astype(BF16)


def _diff_attn(qkv, lam_p, subln, tq, lambda_init):
    b, s, _ = qkv.shape
    return pl.pallas_call(
        functools.partial(_diff_attn_kernel, tq=tq, lambda_init=lambda_init),
        grid=(b, DIFF_HEADS),
        in_specs=[pl.BlockSpec(lam_p.shape, lambda bi, h: (0, 0)),
                  pl.BlockSpec(subln.shape, lambda bi, h: (0, 0)),
                  pl.BlockSpec((None, s, LANES), lambda bi, h: (bi, 0, h)),
                  pl.BlockSpec((None, s, LANES), lambda bi, h: (bi, 0, DIFF_HEADS + h)),
                  pl.BlockSpec((None, s, LANES), lambda bi, h: (bi, 0, 2 * DIFF_HEADS + h))],
        out_specs=pl.BlockSpec((None, s, DIFF_V), lambda bi, h: (bi, 0, h)),
        out_shape=jax.ShapeDtypeStruct((b, s, DIFF_HEADS * DIFF_V), BF16),
        compiler_params=_params("arbitrary", "arbitrary"),
        name="diff_attn",
    )(lam_p, subln, qkv, qkv, qkv)


def _merge_kernel(h_ref, oa_ref, ob_ref, wga_ref, wgb_ref, woa_ref, wob_ref, o_ref):
    h = h_ref[...]
    ya = _dot(oa_ref[...], woa_ref[...])
    yb = _dot(ob_ref[...], wob_ref[...])
    ga = jax.nn.sigmoid(_dot(h, wga_ref[...]))
    gb = jax.nn.sigmoid(_dot(h, wgb_ref[...]))
    o_ref[...] = (ga * ya + gb * yb).astype(BF16)


def _merge(h, oa, ob, wg, woa, wob, tm, tn):
    t, d = h.shape
    nj = d // tn
    row = lambda i, j: (i, 0)
    col = lambda i, j: (0, j)
    return pl.pallas_call(
        _merge_kernel,
        grid=(t // tm, nj),
        in_specs=[pl.BlockSpec((tm, d), row),
                  pl.BlockSpec((tm, oa.shape[1]), row), pl.BlockSpec((tm, ob.shape[1]), row),
                  pl.BlockSpec((d, tn), col), pl.BlockSpec((d, tn), lambda i, j: (0, nj + j)),
                  pl.BlockSpec((woa.shape[0], tn), col), pl.BlockSpec((wob.shape[0], tn), col)],
        out_specs=pl.BlockSpec((tm, tn), lambda i, j: (i, j)),
        out_shape=jax.ShapeDtypeStruct((t, d), BF16),
        compiler_params=_params("arbitrary", "arbitrary"),
        name="gated_merge",
    )(h, oa, ob, wg, wg, woa, wob)


def _outproj_kernel(m_ref, w_ref, x_ref, mod_ref, n2_ref, x1_ref, h2_ref):
    x1 = x_ref[...] + mod_ref[2:3, :] * _dot(m_ref[...], w_ref[...])
    x1_ref[...] = x1
    h2_ref[...] = (_rms(x1, n2_ref[...]) * (1.0 + mod_ref[4:5, :]) + mod_ref[3:4, :]).astype(BF16)


def _outproj(merged, wout, xf, mod3, n2, seq, tm):
    t, d = xf.shape
    per_b = seq // tm
    row = lambda i: (i, 0)
    return pl.pallas_call(
        _outproj_kernel,
        grid=(t // tm,),
        in_specs=[pl.BlockSpec((tm, d), row), pl.BlockSpec((d, d), lambda i: (0, 0)),
                  pl.BlockSpec((tm, d), row),
                  pl.BlockSpec((None, 6, d), lambda i: (i // per_b, 0, 0)),
                  pl.BlockSpec((1, d), lambda i: (0, 0))],
        out_specs=[pl.BlockSpec((tm, d), row), pl.BlockSpec((tm, d), row)],
        out_shape=[jax.ShapeDtypeStruct((t, d), F32), jax.ShapeDtypeStruct((t, d), BF16)],
        compiler_params=_params("arbitrary"),
        name="outproj_norm2",
    )(merged, wout, xf, mod3, n2)


def _sort16_network():
    def merge(lo, hi, r):
        step = 2 * r
        if step < hi - lo:
            yield from merge(lo, hi, step)
            yield from merge(lo + r, hi, step)
            yield from [(i, i + r) for i in range(lo + r, hi - r, step)]
        else:
            yield (lo, lo + r)

    def sort(lo, hi):
        if hi > lo:
            mid = lo + (hi - lo) // 2
            yield from sort(lo, mid)
            yield from sort(mid + 1, hi)
            yield from merge(lo, hi, 1)

    return tuple(sort(0, PEER_TOPK - 1))


SORT16 = _sort16_network()


def _compare_exchange(x, i, j):
    x[i], x[j] = jnp.maximum(x[i], x[j]), jnp.minimum(x[i], x[j])


def _ranked_values(s, out_ref, want_rank):
    n = PEER_TOPK
    groups = [s[SUBLANES * r:SUBLANES * (r + 1), :] for r in range(s.shape[0] // SUBLANES)]
    x = list(groups)
    for i, j in SORT16:
        _compare_exchange(x, i, j)
    for shift in (4, 2, 1):
        x = [jnp.maximum(x[r], pltpu.roll(x[n - 1 - r], shift, 0)) for r in range(n)]
        dist = n // 2
        while dist >= 1:
            for i in range(n):
                if not i & dist:
                    _compare_exchange(x, i, i + dist)
            dist //= 2
    below = [jnp.where(g < x[n - 1], g, NEG_INF) for g in groups]
    while len(below) > 1:
        below = [jnp.maximum(below[2 * i], below[2 * i + 1]) for i in range(len(below) // 2)]
    nxt = jnp.max(below[0], axis=0, keepdims=True)
    out_ref[...] = jnp.full(out_ref.shape, NEG_INF, F32)
    for r in range(n):
        out_ref[r:r + 1, :] = x[r][0:1, :]
    out_ref[n:n + 1, :] = nxt
    if not want_rank:
        return None
    ranks = []
    for g in groups:
        rank = jnp.where(g >= nxt, float(n), float(NRANK))
        for r in range(n - 1, -1, -1):
            rank = jnp.where(g >= x[r], float(r), rank)
        ranks.append(rank)
    return jnp.concatenate(ranks, axis=0)


def _peer_route_kernel(h_ref, wqt_ref, keys_ref, rj_ref, ri_ref, va_ref, vb_ref, vc_ref):
    tm = h_ref.shape[0]
    pq_t = lax.dot_general(wqt_ref[...], h_ref[...], NT_DIMS, preferred_element_type=F32)
    rowi = lax.broadcasted_iota(jnp.int32, (SUBLANES, tm), 0)
    for hd in range(PEER_HEADS):
        sub = []
        rank2 = None
        for p, vref in ((0, va_ref), (1, vb_ref)):
            g = 2 * hd + p
            s = _dot(keys_ref[g], pq_t[g * PEER_HALF:(g + 1) * PEER_HALF, :].astype(BF16))
            sub.append(s)
            tiles = [_ranked_values(s[:, cb * LANES:(cb + 1) * LANES], vref.at[:, pl.ds(cb * LANES, LANES)],
                                    want_rank=(p == 1)) for cb in range(tm // LANES)]
            rank2 = jnp.concatenate(tiles, axis=1) if p == 1 else None
        s1, s2 = sub
        slabs = [va_ref[0:1, :] + vb_ref[...]]
        for k in range(1, SUBLANES):
            n_l = NRANK // (k + 1)
            sl = va_ref[k:k + 1, :] + vb_ref[0:SUBLANES, :]
            slabs.append(sl if n_l >= SUBLANES else jnp.where(rowi < n_l, sl, NEG_INF))
        slabs.append(va_ref[SUBLANES:, :] + vb_ref[0:1, :])
        n_cand = sum(sl.shape[0] for sl in slabs)
        slabs.append(jnp.full((PEER_NKEYS - n_cand, tm), NEG_INF, F32))
        cand = jnp.concatenate(slabs, axis=0)
        for cb in range(tm // LANES):
            _ranked_values(cand[:, cb * LANES:(cb + 1) * LANES], vc_ref.at[:, pl.ds(cb * LANES, LANES)], False)
        c0 = vc_ref[0:1, :]
        z = jnp.sum(jnp.exp(vc_ref[0:PEER_TOPK, :] - c0), axis=0, keepdims=True)
        thr = 0.5 * (vc_ref[PEER_TOPK - 1:PEER_TOPK, :] + vc_ref[PEER_TOPK:PEER_TOPK + 1, :]) - s1
        count = jnp.zeros(thr.shape, F32)
        for r in range(NRANK):
            count = jnp.where(vb_ref[r:r + 1, :] >= thr, float(r + 1), count)
        rj_ref[hd, 0] = pltpu.bitcast(rank2.astype(BF16), jnp.int32)
        rj_ref[hd, 1] = pltpu.bitcast((jnp.exp(s2 - vb_ref[0:1, :]) * (1.0 / z)).astype(BF16), jnp.int32)
        ri_ref[hd, 0] = count
        ri_ref[hd, 1] = 0.5 * jnp.exp(s1 - va_ref[0:1, :])


def _peer_route(h2, wqt, keys_b, tm):
    t, d = h2.shape
    return pl.pallas_call(
        _peer_route_kernel,
        grid=(t // tm,),
        in_specs=[pl.BlockSpec((tm, d), lambda i: (i, 0)),
                  pl.BlockSpec(wqt.shape, lambda i: (0, 0)),
                  pl.BlockSpec(keys_b.shape, lambda i: (0, 0, 0))],
        out_specs=[pl.BlockSpec((PEER_HEADS, 2, PEER_NKEYS // 2, tm), lambda i: (0, 0, 0, i)),
                   pl.BlockSpec((PEER_HEADS, 2, PEER_NKEYS, tm), lambda i: (0, 0, 0, i))],
        out_shape=[jax.ShapeDtypeStruct((PEER_HEADS, 2, PEER_NKEYS // 2, t), jnp.int32),
                   jax.ShapeDtypeStruct((PEER_HEADS, 2, PEER_NKEYS, t), F32)],
        scratch_shapes=[pltpu.VMEM((RANK_ROWS, tm), F32)] * 3,
        compiler_params=_params("arbitrary"),
        name="peer_route",
    )(h2, wqt, keys_b)


def _peer_dense_kernel(h_ref, u_ref, v_ref, rj_ref, ri_ref, x1_ref, mod_ref, o_ref, w_ref):
    e = pl.program_id(1)
    te, tm = w_ref.shape
    pack = 2 * SUBLANES

    @pl.when(e == 0)
    def _():
        o_ref[...] = jnp.zeros(o_ref.shape, F32)

    a_t = lax.dot_general(u_ref[...], h_ref[...], NT_DIMS, preferred_element_type=F32)
    for ii in range(te // PEER_NKEYS):
        for cb in range(tm // LANES):
            cols = pl.ds(cb * LANES, LANES)
            count = [jnp.broadcast_to(ri_ref[hd, 0, ii:ii + 1, cols], (pack, LANES)).astype(BF16)
                     for hd in range(PEER_HEADS)]
            e1 = [jnp.broadcast_to(ri_ref[hd, 1, ii:ii + 1, cols], (pack, LANES)).astype(BF16)
                  for hd in range(PEER_HEADS)]
            for jb in range(PEER_NKEYS // pack):
                rows = pl.ds(jb * SUBLANES, SUBLANES)
                gate = jnp.zeros((pack, LANES), BF16)
                for hd in range(PEER_HEADS):
                    rank2 = pltpu.bitcast(rj_ref[hd, 0, rows, cols], BF16)
                    e2 = pltpu.bitcast(rj_ref[hd, 1, rows, cols], BF16)
                    gate = gate + jnp.where(rank2 < count[hd], e2, jnp.zeros((), BF16)) * e1[hd]
                a = a_t[ii * PEER_NKEYS + jb * pack:ii * PEER_NKEYS + (jb + 1) * pack,
                        cb * LANES:(cb + 1) * LANES]
                gelu2 = a * (1.0 + lax.erf(a * (2.0 ** -0.5)))
                w_ref[pl.ds(ii * PEER_NKEYS + jb * pack, pack), cols] = gate * gelu2.astype(BF16)
    o_ref[...] += lax.dot_general(w_ref[...], v_ref[...], TN_DIMS, preferred_element_type=F32)

    @pl.when(e == pl.num_programs(1) - 1)
    def _():
        o_ref[...] = x1_ref[...] + mod_ref[5:6, :] * o_ref[...]


def _peer_dense(h2, u_b, v_b, route_j, route_i, x1, mod3, seq, tm, te):
    t, d = h2.shape
    n_e = u_b.shape[0]
    per_b = seq // tm
    row = lambda i, e: (i, 0)
    return pl.pallas_call(
        _peer_dense_kernel,
        grid=(t // tm, n_e // te),
        in_specs=[pl.BlockSpec((tm, d), row),
                  pl.BlockSpec((te, d), lambda i, e: (e, 0)),
                  pl.BlockSpec((te, d), lambda i, e: (e, 0)),
                  pl.BlockSpec((PEER_HEADS, 2, PEER_NKEYS // 2, tm), lambda i, e: (0, 0, 0, i)),
                  pl.BlockSpec((PEER_HEADS, 2, te // PEER_NKEYS, tm), lambda i, e: (0, 0, e, i)),
                  pl.BlockSpec((tm, d), row),
                  pl.BlockSpec((None, 6, d), lambda i, e: (i // per_b, 0, 0))],
        out_specs=pl.BlockSpec((tm, d), row),
        out_shape=jax.ShapeDtypeStruct((t, d), F32),
        scratch_shapes=[pltpu.VMEM((te, tm), BF16)],
        compiler_params=_params("arbitrary", "arbitrary"),
        name="peer_dense",
    )(h2, u_b, v_b, route_j, route_i, x1, mod3)


class _Tiles(NamedTuple):
    rows: int
    attn_q: int
    merge_cols: int
    route_tokens: int
    expert_block: int


def _tiles(seq):
    return _Tiles(rows=min(512, seq), attn_q=min(256, seq), merge_cols=1024, route_tokens=min(256, seq),
                  expert_block=8 * PEER_NKEYS)


def _pad_lanes(a, width):
    return jnp.concatenate([a, jnp.zeros(a.shape[:-1] + (width - a.shape[-1],), a.dtype)], axis=-1)


def kernel(x, c, positions, norm1_w, norm2_w, w_ada, b_ada, w_att_in, mla_q_norm, w_mla_qb, mla_kv_norm,
           w_mla_kvb, mla_qk_norm_q, mla_qk_norm_k, w_mla_o, diff_q_norm, diff_k_norm, diff_lambda,
           diff_subln, w_diff_o, w_att_out, w_peer_q, peer_keys, peer_u, peer_v):
    bsz, seq, d = x.shape
    t = bsz * seq
    tiles = _tiles(seq)
    tm, tq = tiles.rows, tiles.attn_q
    xf = x.reshape(t, d)
    pos = positions.reshape(t).astype(F32)

    cos_m, sin_m = _rope_tables(pos, MLA_ROPE)
    z64 = jnp.zeros((t, LANES - MLA_ROPE), F32)
    mla_cos = jnp.concatenate([cos_m, cos_m, z64], axis=-1)
    mla_sin = jnp.concatenate([-sin_m, sin_m, z64], axis=-1)
    cos_d, sin_d = _rope_tables(pos, DIFF_ROT)
    half = DIFF_ROT // 2
    rest = DIFF_D - DIFF_ROT
    z8 = jnp.zeros((t, half), F32)
    diff_cos = jnp.tile(jnp.concatenate([cos_d, cos_d, jnp.ones((t, rest), F32)], axis=-1), (1, 2))
    diff_sa = jnp.tile(jnp.concatenate([-sin_d, z8, jnp.zeros((t, rest), F32)], axis=-1), (1, 2))
    diff_sb = jnp.tile(jnp.concatenate([z8, sin_d, jnp.zeros((t, rest), F32)], axis=-1), (1, 2))

    n_lat = MLA_Q_LORA + MLA_KV_LORA + MLA_ROPE
    n_diff = 3 * DIFF_HEADS * DIFF_V
    for l in range(w_ada.shape[0]):
        lambda_init = 0.8 - 0.6 * math.exp(-0.3 * l)
        w_in = w_att_in[l].astype(BF16)
        w1 = _pad_lanes(w_in[:, :n_lat], n_lat + LANES - MLA_ROPE)
        w2 = w_in[:, n_lat:n_lat + n_diff]
        wg = w_in[:, n_lat + n_diff:]
        wqb = w_mla_qb[l].astype(BF16).reshape(MLA_Q_LORA, MLA_HEADS, MLA_QK)
        wqn = wqb[:, :, :MLA_NOPE].reshape(MLA_Q_LORA, MLA_HEADS * MLA_NOPE)
        wqr = _pad_lanes(wqb[:, :, MLA_NOPE:], LANES).reshape(MLA_Q_LORA, MLA_HEADS * LANES)
        wkvb = w_mla_kvb[l].astype(BF16).reshape(MLA_KV_LORA, MLA_HEADS, MLA_NOPE + MLA_V)
        wkn = wkvb[:, :, :MLA_NOPE].reshape(MLA_KV_LORA, MLA_HEADS * MLA_NOPE)
        wv = wkvb[:, :, MLA_NOPE:].reshape(MLA_KV_LORA, MLA_HEADS * MLA_V)
        gq = _pad_lanes(mla_qk_norm_q[l].reshape(1, MLA_QK), 2 * LANES)
        gk = _pad_lanes(mla_qk_norm_k[l].reshape(1, MLA_QK), 2 * LANES)
        g3 = jnp.stack([jnp.tile(diff_q_norm[l], 2) * (DIFF_D ** -0.5 * LOG2E),
                        jnp.tile(diff_k_norm[l], 2),
                        jnp.ones((LANES,), F32)]).reshape(3, 1, LANES)

        mod3 = _modulation(c, w_ada[l], b_ada[l]).reshape(bsz, 6, d)

        h, q, k, v = _mla_prep(xf, mod3, norm1_w[l].reshape(1, d), w1,
                               mla_q_norm[l].reshape(1, -1), mla_kv_norm[l].reshape(1, -1),
                               wqn, wqr, wkn, wv, gq, gk, mla_cos, mla_sin, seq, tm)
        dqkv = _diff_prep(h, w2, g3, diff_cos, diff_sa, diff_sb, tm)
        o_a = _mla_attn(q.reshape(bsz, seq, -1), k.reshape(bsz, seq, -1), v.reshape(bsz, seq, -1), tq)
        o_b = _diff_attn(dqkv.reshape(bsz, seq, -1), diff_lambda[l], diff_subln[l].reshape(1, DIFF_V),
                         tq, lambda_init)
        merged = _merge(h, o_a.reshape(t, -1), o_b.reshape(t, -1), wg, w_mla_o[l].astype(BF16),
                        w_diff_o[l].astype(BF16), tm, tiles.merge_cols)
        x1, h2 = _outproj(merged, w_att_out[l].astype(BF16), xf, mod3, norm2_w[l].reshape(1, d), seq, tm)

        route_j, route_i = _peer_route(h2, w_peer_q[l].T.astype(BF16),
                                       peer_keys[l].reshape(2 * PEER_HEADS, PEER_NKEYS, PEER_HALF).astype(BF16),
                                       tiles.route_tokens)
        xf = _peer_dense(h2, peer_u[l].astype(BF16), peer_v[l].astype(BF16), route_j, route_i, x1, mod3,
                         seq, tm, tiles.expert_block)
    return xf.reshape(bsz, seq, d)
```

```python
import functools
import math
from typing import NamedTuple

import jax
import jax.numpy as jnp
from jax import lax
from jax.experimental import pallas as pl
from jax.experimental.pallas import tpu as pltpu

F32 = jnp.float32
BF16 = jnp.bfloat16

ROPE_THETA = 500000.0
EPS = 1e-6
MLA_HEADS = 8
MLA_Q_LORA = 512
MLA_KV_LORA = 512
MLA_NOPE = 128
MLA_ROPE = 64
MLA_V = 128
MLA_QK = MLA_NOPE + MLA_ROPE
DIFF_HEADS = 8
DIFF_D = 64
DIFF_V = 2 * DIFF_D
DIFF_ROT = DIFF_D // 4
PEER_HEADS = 8
PEER_NKEYS = 128
PEER_HALF = 128
PEER_TOPK = 16

LANES = 128
SUBLANES = 8
VMEM_LIMIT_BYTES = 56 * 1024 * 1024
LOG2E = 1.4426950408889634
NEG_BIG = float(jnp.finfo(jnp.float32).min)
NEG_INF = float("-inf")
NT_DIMS = (((1,), (1,)), ((), ()))
TN_DIMS = (((0,), (0,)), ((), ()))

NRANK = PEER_TOPK + 1
RANK_ROWS = 24


def _params(*sem, flags=None):
    return pltpu.CompilerParams(dimension_semantics=sem, vmem_limit_bytes=VMEM_LIMIT_BYTES, flags=flags)


def _rms(x, w):
    return x * lax.rsqrt(jnp.mean(x * x, axis=-1, keepdims=True) + EPS) * w


def _dot(a, b):
    return jnp.dot(a, b, preferred_element_type=F32)


def _mod_kernel(c_ref, w_ref, b_ref, o_ref):
    c = c_ref[...]
    a = (c * jax.nn.sigmoid(c)).astype(BF16)
    o_ref[...] = _dot(a, w_ref[...].astype(BF16)) + b_ref[...]


def _modulation(c, w, b):
    bsz, d = c.shape
    n = w.shape[1]
    tn = 1536
    return pl.pallas_call(
        _mod_kernel,
        grid=(n // tn,),
        in_specs=[pl.BlockSpec((bsz, d), lambda j: (0, 0)),
                  pl.BlockSpec((d, tn), lambda j: (0, j)),
                  pl.BlockSpec((1, tn), lambda j: (0, j))],
        out_specs=pl.BlockSpec((bsz, tn), lambda j: (0, j)),
        out_shape=jax.ShapeDtypeStruct((bsz, n), F32),
        compiler_params=_params("arbitrary"),
        name="adaln_mod",
    )(c, w, b.reshape(1, n))


def _trig_kernel(p_ref, inv_ref, cos_ref, sin_ref):
    ang = p_ref[...] * inv_ref[...]
    cos_ref[...] = jnp.cos(ang)
    sin_ref[...] = jnp.sin(ang)


def _rope_tables(pos_f32, r):
    t = pos_f32.shape[0]
    nf = r // 2
    inv = ROPE_THETA ** (-jnp.arange(0, r, 2, dtype=F32) / r)
    rep = LANES // nf
    rows = t // rep
    p = jnp.broadcast_to(pos_f32[:, None], (t, nf)).reshape(rows, LANES)
    inv_row = jnp.tile(inv, rep).reshape(1, LANES)
    tr = min(rows, 1024)
    cos, sin = pl.pallas_call(
        _trig_kernel,
        grid=(rows // tr,),
        in_specs=[pl.BlockSpec((tr, LANES), lambda i: (i, 0)),
                  pl.BlockSpec((1, LANES), lambda i: (0, 0))],
        out_specs=[pl.BlockSpec((tr, LANES), lambda i: (i, 0))] * 2,
        out_shape=[jax.ShapeDtypeStruct((rows, LANES), F32)] * 2,
        compiler_params=_params("arbitrary"),
        name="rope_tables",
    )(p, inv_row)
    return cos.reshape(t, nf), sin.reshape(t, nf)


def _mla_prep_kernel(x_ref, mod_ref, n1_ref, w1_ref, qn_ref, kvn_ref, wqn_ref, wqr_ref, wkn_ref,
                     wv_ref, gq_ref, gk_ref, c_ref, s_ref, h_ref, q_ref, k_ref, v_ref, *, qscale):
    x = x_ref[...]
    h = _rms(x, n1_ref[...]) * (1.0 + mod_ref[1:2, :]) + mod_ref[0:1, :]
    hb = h.astype(BF16)
    h_ref[...] = hb
    lat = _dot(hb, w1_ref[...])
    qn = _rms(lat[:, :MLA_Q_LORA], qn_ref[...]).astype(BF16)
    kvn = _rms(lat[:, MLA_Q_LORA:MLA_Q_LORA + MLA_KV_LORA], kvn_ref[...]).astype(BF16)
    kr = lat[:, MLA_Q_LORA + MLA_KV_LORA:]
    q_nope = _dot(qn, wqn_ref[...])
    q_rope = _dot(qn, wqr_ref[...])
    k_nope = _dot(kvn, wkn_ref[...])
    v_ref[...] = _dot(kvn, wv_ref[...]).astype(BF16)
    cos = c_ref[...]
    sin = s_ref[...]
    gq = gq_ref[...]
    gk = gk_ref[...]
    kr_ss = jnp.sum(kr * kr, axis=-1, keepdims=True)

    def rot(r):
        return r * cos + (pltpu.roll(r, LANES - 32, 1) + pltpu.roll(r, 32, 1)) * sin

    kr_rot = rot(kr * gk[:, LANES:])
    for hd in range(MLA_HEADS):
        lo, hi = hd * LANES, (hd + 1) * LANES
        a, r = q_nope[:, lo:hi], q_rope[:, lo:hi]
        ss = jnp.sum(a * a + r * r, axis=-1, keepdims=True)
        inv = lax.rsqrt(ss * (1.0 / MLA_QK) + EPS) * qscale
        q_ref[:, 2 * lo:2 * lo + LANES] = (a * inv * gq[:, :LANES]).astype(BF16)
        q_ref[:, 2 * lo + LANES:2 * hi] = (rot(r * gq[:, LANES:]) * inv).astype(BF16)
        a = k_nope[:, lo:hi]
        ss = jnp.sum(a * a, axis=-1, keepdims=True) + kr_ss
        inv = lax.rsqrt(ss * (1.0 / MLA_QK) + EPS)
        k_ref[:, 2 * lo:2 * lo + LANES] = (a * inv * gk[:, :LANES]).astype(BF16)
        k_ref[:, 2 * lo + LANES:2 * hi] = (kr_rot * inv).astype(BF16)


def _mla_prep(xf, mod3, n1, w1, qnw, kvnw, wqn, wqr, wkn, wv, gq, gk, cos_t, sin_t, seq, tm):
    t, d = xf.shape
    per_b = seq // tm
    const = lambda i: (0, 0)
    row = lambda i: (i, 0)
    full = lambda a: pl.BlockSpec(a.shape, const)
    return pl.pallas_call(
        functools.partial(_mla_prep_kernel, qscale=MLA_QK ** -0.5 * LOG2E),
        grid=(t // tm,),
        in_specs=[pl.BlockSpec((tm, d), row),
                  pl.BlockSpec((None, 6, d), lambda i: (i // per_b, 0, 0)),
                  full(n1), full(w1), full(qnw), full(kvnw), full(wqn), full(wqr), full(wkn), full(wv),
                  full(gq), full(gk),
                  pl.BlockSpec((tm, LANES), row), pl.BlockSpec((tm, LANES), row)],
        out_specs=[pl.BlockSpec((tm, d), row),
                   pl.BlockSpec((tm, MLA_HEADS * 2 * LANES), row),
                   pl.BlockSpec((tm, MLA_HEADS * 2 * LANES), row),
                   pl.BlockSpec((tm, MLA_HEADS * MLA_V), row)],
        out_shape=[jax.ShapeDtypeStruct((t, d), BF16),
                   jax.ShapeDtypeStruct((t, MLA_HEADS * 2 * LANES), BF16),
                   jax.ShapeDtypeStruct((t, MLA_HEADS * 2 * LANES), BF16),
                   jax.ShapeDtypeStruct((t, MLA_HEADS * MLA_V), BF16)],
        compiler_params=_params("arbitrary"),
        name="mla_prep",
    )(xf, mod3, n1, w1, qnw, kvnw, wqn, wqr, wkn, wv, gq, gk, cos_t, sin_t)


def _diff_prep_kernel(h_ref, w_ref, g_ref, c_ref, sa_ref, sb_ref, seg_ref, segt_ref, o_ref):
    j = pl.program_id(1)
    d = _dot(h_ref[...], w_ref[...])

    @pl.when(j == 2)
    def _():
        o_ref[...] = d.astype(BF16)

    @pl.when(j < 2)
    def _():
        g = g_ref[...]
        cos, sa, sb = c_ref[...], sa_ref[...], sb_ref[...]
        ssq = _dot((d * d).astype(BF16), seg_ref[...])
        inv = lax.rsqrt(ssq * (1.0 / DIFF_D) + EPS)
        inv_hi = inv.astype(BF16)
        inv_lo = (inv - inv_hi.astype(F32)).astype(BF16)
        inv_full = _dot(jnp.concatenate([inv_hi, inv_lo], axis=1), segt_ref[...])
        for cb in range(d.shape[1] // LANES):
            y = d[:, cb * LANES:(cb + 1) * LANES] * inv_full[:, cb * LANES:(cb + 1) * LANES] * g
            y = y * cos + pltpu.roll(y, LANES - DIFF_ROT // 2, 1) * sa + pltpu.roll(y, DIFF_ROT // 2, 1) * sb
            o_ref[:, cb * LANES:(cb + 1) * LANES] = y.astype(BF16)


def _diff_prep(h, w2, g3, cos_t, sa_t, sb_t, tm):
    t, d = h.shape
    n = w2.shape[1]
    tn = n // 3
    row = lambda i, j: (i, 0)
    const = lambda i, j: (0, 0)
    seg = (jnp.arange(tn)[:, None] // DIFF_D == jnp.arange(LANES)[None, :]).astype(BF16)
    segt = jnp.concatenate([seg.T, seg.T], axis=0)
    return pl.pallas_call(
        _diff_prep_kernel,
        grid=(t // tm, 3),
        in_specs=[pl.BlockSpec((tm, d), row),
                  pl.BlockSpec((d, tn), lambda i, j: (0, j)),
                  pl.BlockSpec((None, 1, LANES), lambda i, j: (j, 0, 0)),
                  pl.BlockSpec((tm, LANES), row), pl.BlockSpec((tm, LANES), row),
                  pl.BlockSpec((tm, LANES), row),
                  pl.BlockSpec(seg.shape, const), pl.BlockSpec(segt.shape, const)],
        out_specs=pl.BlockSpec((tm, tn), lambda i, j: (i, j)),
        out_shape=jax.ShapeDtypeStruct((t, n), BF16),
        compiler_params=_params("arbitrary", "arbitrary"),
        name="diff_prep",
    )(h, w2, g3, cos_t, sa_t, sb_t, seg, segt)


def _causal_probs(q, k_ref, i, tq):
    sd = lax.dot_general(q, k_ref[i * tq:(i + 1) * tq, :], NT_DIMS, preferred_element_type=F32)
    rowi = lax.broadcasted_iota(jnp.int32, (tq, tq), 0)
    coli = lax.broadcasted_iota(jnp.int32, (tq, tq), 1)
    sd = jnp.where(coli <= rowi, sd, NEG_BIG)
    m = jnp.max(sd, axis=-1, keepdims=True)
    sp = None
    if i > 0:
        sp = lax.dot_general(q, k_ref[0:i * tq, :], NT_DIMS, preferred_element_type=F32)
        m = jnp.maximum(m, jnp.max(sp, axis=-1, keepdims=True))
    pd = jnp.exp2(sd - m)
    l = jnp.sum(pd, axis=-1, keepdims=True)
    pp = None
    if i > 0:
        pp = jnp.exp2(sp - m)
        l = l + jnp.sum(pp, axis=-1, keepdims=True)
    return pd, pp, l


def _pv(pd, pp, v_ref, i, tq):
    o = _dot(pd.astype(BF16), v_ref[i * tq:(i + 1) * tq, :])
    if pp is not None:
        o = o + _dot(pp.astype(BF16), v_ref[0:i * tq, :])
    return o


def _mla_attn_kernel(q_ref, k_ref, v_ref, o_ref, *, tq):
    for i in range(q_ref.shape[0] // tq):
        pd, pp, l = _causal_probs(q_ref[i * tq:(i + 1) * tq, :], k_ref, i, tq)
        o = _pv(pd, pp, v_ref, i, tq)
        o_ref[i * tq:(i + 1) * tq, :] = (o * (1.0 / l)).astype(BF16)


def _mla_attn(q, k, v, tq):
    b, s, _ = q.shape
    qk = lambda bi, h: (bi, 0, h)
    return pl.pallas_call(
        functools.partial(_mla_attn_kernel, tq=tq),
        grid=(b, MLA_HEADS),
        in_specs=[pl.BlockSpec((None, s, 2 * LANES), qk), pl.BlockSpec((None, s, 2 * LANES), qk),
                  pl.BlockSpec((None, s, MLA_V), qk)],
        out_specs=pl.BlockSpec((None, s, MLA_V), qk),
        out_shape=jax.ShapeDtypeStruct((b, s, MLA_HEADS * MLA_V), BF16),
        compiler_params=_params("arbitrary", "arbitrary"),
        name="mla_attn",
    )(q, k, v)


def _diff_attn_kernel(lam_ref, g_ref, q_ref, k_ref, v_ref, o_ref, *, tq, lambda_init):
    lp = lam_ref[...]
    lam = (jnp.exp(jnp.sum(lp[0:1] * lp[1:2], axis=-1, keepdims=True))
           - jnp.exp(jnp.sum(lp[2:3] * lp[3:4], axis=-1, keepdims=True)) + lambda_init)
    g = g_ref[...] * (1.0 - lambda_init)
    lo = lax.broadcasted_iota(jnp.int32, (tq, LANES), 1) < DIFF_D
    for i in range(q_ref.shape[0] // tq):
        q = q_ref[i * tq:(i + 1) * tq, :]
        zero = jnp.zeros_like(q)
        pd1, pp1, l1 = _causal_probs(jnp.where(lo, q, zero), k_ref, i, tq)
        pd2, pp2, l2 = _causal_probs(jnp.where(lo, zero, q), k_ref, i, tq)
        o = _pv(pd1, pp1, v_ref, i, tq) * (1.0 / l1) - _pv(pd2, pp2, v_ref, i, tq) * (lam / l2)
        o_ref[i * tq:(i + 1) * tq, :] = _rms(o, g).astype(BF16)


def _diff_attn(qkv, lam_p, subln, tq, lambda_init):
    b, s, _ = qkv.shape
    return pl.pallas_call(
        functools.partial(_diff_attn_kernel, tq=tq, lambda_init=lambda_init),
        grid=(b, DIFF_HEADS),
        in_specs=[pl.BlockSpec(lam_p.shape, lambda bi, h: (0, 0)),
                  pl.BlockSpec(subln.shape, lambda bi, h: (0, 0)),
                  pl.BlockSpec((None, s, LANES), lambda bi, h: (bi, 0, h)),
                  pl.BlockSpec((None, s, LANES), lambda bi, h: (bi, 0, DIFF_HEADS + h)),
                  pl.BlockSpec((None, s, LANES), lambda bi, h: (bi, 0, 2 * DIFF_HEADS + h))],
        out_specs=pl.BlockSpec((None, s, DIFF_V), lambda bi, h: (bi, 0, h)),
        out_shape=jax.ShapeDtypeStruct((b, s, DIFF_HEADS * DIFF_V), BF16),
        compiler_params=_params("arbitrary", "arbitrary"),
        name="diff_attn",
    )(lam_p, subln, qkv, qkv, qkv)


def _merge_kernel(h_ref, oa_ref, ob_ref, wga_ref, wgb_ref, woa_ref, wob_ref, o_ref):
    h = h_ref[...]
    ya = _dot(oa_ref[...], woa_ref[...])
    yb = _dot(ob_ref[...], wob_ref[...])
    ga = jax.nn.sigmoid(_dot(h, wga_ref[...]))
    gb = jax.nn.sigmoid(_dot(h, wgb_ref[...]))
    o_ref[...] = (ga * ya + gb * yb).astype(BF16)


def _merge(h, oa, ob, wg, woa, wob, tm, tn):
    t, d = h.shape
    nj = d // tn
    row = lambda i, j: (i, 0)
    col = lambda i, j: (0, j)
    return pl.pallas_call(
        _merge_kernel,
        grid=(t // tm, nj),
        in_specs=[pl.BlockSpec((tm, d), row),
                  pl.BlockSpec((tm, oa.shape[1]), row), pl.BlockSpec((tm, ob.shape[1]), row),
                  pl.BlockSpec((d, tn), col), pl.BlockSpec((d, tn), lambda i, j: (0, nj + j)),
                  pl.BlockSpec((woa.shape[0], tn), col), pl.BlockSpec((wob.shape[0], tn), col)],
        out_specs=pl.BlockSpec((tm, tn), lambda i, j: (i, j)),
        out_shape=jax.ShapeDtypeStruct((t, d), BF16),
        compiler_params=_params("arbitrary", "arbitrary"),
        name="gated_merge",
    )(h, oa, ob, wg, wg, woa, wob)


def _outproj_kernel(m_ref, w_ref, x_ref, mod_ref, n2_ref, x1_ref, h2_ref):
    x1 = x_ref[...] + mod_ref[2:3, :] * _dot(m_ref[...], w_ref[...])
    x1_ref[...] = x1
    h2_ref[...] = (_rms(x1, n2_ref[...]) * (1.0 + mod_ref[4:5, :]) + mod_ref[3:4, :]).astype(BF16)


def _outproj(merged, wout, xf, mod3, n2, seq, tm):
    t, d = xf.shape
    per_b = seq // tm
    row = lambda i: (i, 0)
    return pl.pallas_call(
        _outproj_kernel,
        grid=(t // tm,),
        in_specs=[pl.BlockSpec((tm, d), row), pl.BlockSpec((d, d), lambda i: (0, 0)),
                  pl.BlockSpec((tm, d), row),
                  pl.BlockSpec((None, 6, d), lambda i: (i // per_b, 0, 0)),
                  pl.BlockSpec((1, d), lambda i: (0, 0))],
        out_specs=[pl.BlockSpec((tm, d), row), pl.BlockSpec((tm, d), row)],
        out_shape=[jax.ShapeDtypeStruct((t, d), F32), jax.ShapeDtypeStruct((t, d), BF16)],
        compiler_params=_params("arbitrary"),
        name="outproj_norm2",
    )(merged, wout, xf, mod3, n2)


def _sort16_network():
    def merge(lo, hi, r):
        step = 2 * r
        if step < hi - lo:
            yield from merge(lo, hi, step)
            yield from merge(lo + r, hi, step)
            yield from [(i, i + r) for i in range(lo + r, hi - r, step)]
        else:
            yield (lo, lo + r)

    def sort(lo, hi):
        if hi > lo:
            mid = lo + (hi - lo) // 2
            yield from sort(lo, mid)
            yield from sort(mid + 1, hi)
            yield from merge(lo, hi, 1)

    return tuple(sort(0, PEER_TOPK - 1))


SORT16 = _sort16_network()


def _compare_exchange(x, i, j):
    x[i], x[j] = jnp.maximum(x[i], x[j]), jnp.minimum(x[i], x[j])


def _ranked_values(s, out_ref, want_rank):
    n = PEER_TOPK
    groups = [s[SUBLANES * r:SUBLANES * (r + 1), :] for r in range(s.shape[0] // SUBLANES)]
    x = list(groups)
    for i, j in SORT16:
        _compare_exchange(x, i, j)
    for shift in (4, 2, 1):
        x = [jnp.maximum(x[r], pltpu.roll(x[n - 1 - r], shift, 0)) for r in range(n)]
        dist = n // 2
        while dist >= 1:
            for i in range(n):
                if not i & dist:
                    _compare_exchange(x, i, i + dist)
            dist //= 2
    below = [jnp.where(g < x[n - 1], g, NEG_INF) for g in groups]
    while len(below) > 1:
        below = [jnp.maximum(below[2 * i], below[2 * i + 1]) for i in range(len(below) // 2)]
    nxt = jnp.max(below[0], axis=0, keepdims=True)
    out_ref[...] = jnp.full(out_ref.shape, NEG_INF, F32)
    for r in range(n):
        out_ref[r:r + 1, :] = x[r][0:1, :]
    out_ref[n:n + 1, :] = nxt
    if not want_rank:
        return None
    ranks = []
    for g in groups:
        rank = jnp.where(g >= nxt, float(n), float(NRANK))
        for r in range(n - 1, -1, -1):
            rank = jnp.where(g >= x[r], float(r), rank)
        ranks.append(rank)
    return jnp.concatenate(ranks, axis=0)


def _peer_route_kernel(h_ref, wqt_ref, keys_ref, rj_ref, ri_ref, va_ref, vb_ref, vc_ref):
    tm = h_ref.shape[0]
    pq_t = lax.dot_general(wqt_ref[...], h_ref[...], NT_DIMS, preferred_element_type=F32)
    rowi = lax.broadcasted_iota(jnp.int32, (SUBLANES, tm), 0)
    for hd in range(PEER_HEADS):
        sub = []
        rank2 = None
        for p, vref in ((0, va_ref), (1, vb_ref)):
            g = 2 * hd + p
            s = _dot(keys_ref[g], pq_t[g * PEER_HALF:(g + 1) * PEER_HALF, :].astype(BF16))
            sub.append(s)
            tiles = [_ranked_values(s[:, cb * LANES:(cb + 1) * LANES], vref.at[:, pl.ds(cb * LANES, LANES)],
                                    want_rank=(p == 1)) for cb in range(tm // LANES)]
            rank2 = jnp.concatenate(tiles, axis=1) if p == 1 else None
        s1, s2 = sub
        slabs = [va_ref[0:1, :] + vb_ref[...]]
        for k in range(1, SUBLANES):
            n_l = NRANK // (k + 1)
            sl = va_ref[k:k + 1, :] + vb_ref[0:SUBLANES, :]
            slabs.append(sl if n_l >= SUBLANES else jnp.where(rowi < n_l, sl, NEG_INF))
        slabs.append(va_ref[SUBLANES:, :] + vb_ref[0:1, :])
        n_cand = sum(sl.shape[0] for sl in slabs)
        slabs.append(jnp.full((PEER_NKEYS - n_cand, tm), NEG_INF, F32))
        cand = jnp.concatenate(slabs, axis=0)
        for cb in range(tm // LANES):
            _ranked_values(cand[:, cb * LANES:(cb + 1) * LANES], vc_ref.at[:, pl.ds(cb * LANES, LANES)], False)
        c0 = vc_ref[0:1, :]
        z = jnp.sum(jnp.exp(vc_ref[0:PEER_TOPK, :] - c0), axis=0, keepdims=True)
        thr = 0.5 * (vc_ref[PEER_TOPK - 1:PEER_TOPK, :] + vc_ref[PEER_TOPK:PEER_TOPK + 1, :]) - s1
        count = jnp.zeros(thr.shape, F32)
        for r in range(NRANK):
            count = jnp.where(vb_ref[r:r + 1, :] >= thr, float(r + 1), count)
        rj_ref[hd, 0] = pltpu.bitcast(rank2.astype(BF16), jnp.int32)
        rj_ref[hd, 1] = pltpu.bitcast((jnp.exp(s2 - vb_ref[0:1, :]) * (1.0 / z)).astype(BF16), jnp.int32)
        ri_ref[hd, 0] = count
        ri_ref[hd, 1] = 0.5 * jnp.exp(s1 - va_ref[0:1, :])


def _peer_route(h2, wqt, keys_b, tm):
    t, d = h2.shape
    return pl.pallas_call(
        _peer_route_kernel,
        grid=(t // tm,),
        in_specs=[pl.BlockSpec((tm, d), lambda i: (i, 0)),
                  pl.BlockSpec(wqt.shape, lambda i: (0, 0)),
                  pl.BlockSpec(keys_b.shape, lambda i: (0, 0, 0))],
        out_specs=[pl.BlockSpec((PEER_HEADS, 2, PEER_NKEYS // 2, tm), lambda i: (0, 0, 0, i)),
                   pl.BlockSpec((PEER_HEADS, 2, PEER_NKEYS, tm), lambda i: (0, 0, 0, i))],
        out_shape=[jax.ShapeDtypeStruct((PEER_HEADS, 2, PEER_NKEYS // 2, t), jnp.int32),
                   jax.ShapeDtypeStruct((PEER_HEADS, 2, PEER_NKEYS, t), F32)],
        scratch_shapes=[pltpu.VMEM((RANK_ROWS, tm), F32)] * 3,
        compiler_params=_params("arbitrary"),
        name="peer_route",
    )(h2, wqt, keys_b)


def _peer_dense_kernel(h_ref, u_ref, v_ref, rj_ref, ri_ref, x1_ref, mod_ref, o_ref, w_ref):
    e = pl.program_id(1)
    te, tm = w_ref.shape
    pack = 2 * SUBLANES

    @pl.when(e == 0)
    def _():
        o_ref[...] = jnp.zeros(o_ref.shape, F32)

    a_t = lax.dot_general(u_ref[...], h_ref[...], NT_DIMS, preferred_element_type=F32)
    for ii in range(te // PEER_NKEYS):
        for cb in range(tm // LANES):
            cols = pl.ds(cb * LANES, LANES)
            count = [jnp.broadcast_to(ri_ref[hd, 0, ii:ii + 1, cols], (pack, LANES)).astype(BF16)
                     for hd in range(PEER_HEADS)]
            e1 = [jnp.broadcast_to(ri_ref[hd, 1, ii:ii + 1, cols], (pack, LANES)).astype(BF16)
                  for hd in range(PEER_HEADS)]
            for jb in range(PEER_NKEYS // pack):
                rows = pl.ds(jb * SUBLANES, SUBLANES)
                gate = jnp.zeros((pack, LANES), BF16)
                for hd in range(PEER_HEADS):
                    rank2 = pltpu.bitcast(rj_ref[hd, 0, rows, cols], BF16)
                    e2 = pltpu.bitcast(rj_ref[hd, 1, rows, cols], BF16)
                    gate = gate + jnp.where(rank2 < count[hd], e2, jnp.zeros((), BF16)) * e1[hd]
                a = a_t[ii * PEER_NKEYS + jb * pack:ii * PEER_NKEYS + (jb + 1) * pack,
                        cb * LANES:(cb + 1) * LANES]
                gelu2 = a * (1.0 + lax.erf(a * (2.0 ** -0.5)))
                w_ref[pl.ds(ii * PEER_NKEYS + jb * pack, pack), cols] = gate * gelu2.astype(BF16)
    o_ref[...] += lax.dot_general(w_ref[...], v_ref[...], TN_DIMS, preferred_element_type=F32)

    @pl.when(e == pl.num_programs(1) - 1)
    def _():
        o_ref[...] = x1_ref[...] + mod_ref[5:6, :] * o_ref[...]


def _peer_dense(h2, u_b, v_b, route_j, route_i, x1, mod3, seq, tm, te):
    t, d = h2.shape
    n_e = u_b.shape[0]
    per_b = seq // tm
    row = lambda i, e: (i, 0)
    return pl.pallas_call(
        _peer_dense_kernel,
        grid=(t // tm, n_e // te),
        in_specs=[pl.BlockSpec((tm, d), row),
                  pl.BlockSpec((te, d), lambda i, e: (e, 0)),
                  pl.BlockSpec((te, d), lambda i, e: (e, 0)),
                  pl.BlockSpec((PEER_HEADS, 2, PEER_NKEYS // 2, tm), lambda i, e: (0, 0, 0, i)),
                  pl.BlockSpec((PEER_HEADS, 2, te // PEER_NKEYS, tm), lambda i, e: (0, 0, e, i)),
                  pl.BlockSpec((tm, d), row),
                  pl.BlockSpec((None, 6, d), lambda i, e: (i // per_b, 0, 0))],
        out_specs=pl.BlockSpec((tm, d), row),
        out_shape=jax.ShapeDtypeStruct((t, d), F32),
        scratch_shapes=[pltpu.VMEM((te, tm), BF16)],
        compiler_params=_params("arbitrary", "arbitrary"),
        name="peer_dense",
    )(h2, u_b, v_b, route_j, route_i, x1, mod3)


class _Tiles(NamedTuple):
    rows: int
    attn_q: int
    merge_cols: int
    route_tokens: int
    expert_block: int


def _tiles(seq):
    return _Tiles(rows=min(512, seq), attn_q=min(256, seq), merge_cols=1024, route_tokens=min(256, seq),
                  expert_block=8 * PEER_NKEYS)


def _pad_lanes(a, width):
    return jnp.concatenate([a, jnp.zeros(a.shape[:-1] + (width - a.shape[-1],), a.dtype)], axis=-1)


def kernel(x, c, positions, norm1_w, norm2_w, w_ada, b_ada, w_att_in, mla_q_norm, w_mla_qb, mla_kv_norm,
           w_mla_kvb, mla_qk_norm_q, mla_qk_norm_k, w_mla_o, diff_q_norm, diff_k_norm, diff_lambda,
           diff_subln, w_diff_o, w_att_out, w_peer_q, peer_keys, peer_u, peer_v):
    bsz, seq, d = x.shape
    t = bsz * seq
    tiles = _tiles(seq)
    tm, tq = tiles.rows, tiles.attn_q
    xf = x.reshape(t, d)
    pos = positions.reshape(t).astype(F32)

    cos_m, sin_m = _rope_tables(pos, MLA_ROPE)
    z64 = jnp.zeros((t, LANES - MLA_ROPE), F32)
    mla_cos = jnp.concatenate([cos_m, cos_m, z64], axis=-1)
    mla_sin = jnp.concatenate([-sin_m, sin_m, z64], axis=-1)
    cos_d, sin_d = _rope_tables(pos, DIFF_ROT)
    half = DIFF_ROT // 2
    rest = DIFF_D - DIFF_ROT
    z8 = jnp.zeros((t, half), F32)
    diff_cos = jnp.tile(jnp.concatenate([cos_d, cos_d, jnp.ones((t, rest), F32)], axis=-1), (1, 2))
    diff_sa = jnp.tile(jnp.concatenate([-sin_d, z8, jnp.zeros((t, rest), F32)], axis=-1), (1, 2))
    diff_sb = jnp.tile(jnp.concatenate([z8, sin_d, jnp.zeros((t, rest), F32)], axis=-1), (1, 2))

    n_lat = MLA_Q_LORA + MLA_KV_LORA + MLA_ROPE
    n_diff = 3 * DIFF_HEADS * DIFF_V
    for l in range(w_ada.shape[0]):
        lambda_init = 0.8 - 0.6 * math.exp(-0.3 * l)
        w_in = w_att_in[l].astype(BF16)
        w1 = _pad_lanes(w_in[:, :n_lat], n_lat + LANES - MLA_ROPE)
        w2 = w_in[:, n_lat:n_lat + n_diff]
        wg = w_in[:, n_lat + n_diff:]
        wqb = w_mla_qb[l].astype(BF16).reshape(MLA_Q_LORA, MLA_HEADS, MLA_QK)
        wqn = wqb[:, :, :MLA_NOPE].reshape(MLA_Q_LORA, MLA_HEADS * MLA_NOPE)
        wqr = _pad_lanes(wqb[:, :, MLA_NOPE:], LANES).reshape(MLA_Q_LORA, MLA_HEADS * LANES)
        wkvb = w_mla_kvb[l].astype(BF16).reshape(MLA_KV_LORA, MLA_HEADS, MLA_NOPE + MLA_V)
        wkn = wkvb[:, :, :MLA_NOPE].reshape(MLA_KV_LORA, MLA_HEADS * MLA_NOPE)
        wv = wkvb[:, :, MLA_NOPE:].reshape(MLA_KV_LORA, MLA_HEADS * MLA_V)
        gq = _pad_lanes(mla_qk_norm_q[l].reshape(1, MLA_QK), 2 * LANES)
        gk = _pad_lanes(mla_qk_norm_k[l].reshape(1, MLA_QK), 2 * LANES)
        g3 = jnp.stack([jnp.tile(diff_q_norm[l], 2) * (DIFF_D ** -0.5 * LOG2E),
                        jnp.tile(diff_k_norm[l], 2),
                        jnp.ones((LANES,), F32)]).reshape(3, 1, LANES)

        mod3 = _modulation(c, w_ada[l], b_ada[l]).reshape(bsz, 6, d)

        h, q, k, v = _mla_prep(xf, mod3, norm1_w[l].reshape(1, d), w1,
                               mla_q_norm[l].reshape(1, -1), mla_kv_norm[l].reshape(1, -1),
                               wqn, wqr, wkn, wv, gq, gk, mla_cos, mla_sin, seq, tm)
        dqkv = _diff_prep(h, w2, g3, diff_cos, diff_sa, diff_sb, min(2 * tm, seq))
        o_a = _mla_attn(q.reshape(bsz, seq, -1), k.reshape(bsz, seq, -1), v.reshape(bsz, seq, -1), tq)
        o_b = _diff_attn(dqkv.reshape(bsz, seq, -1), diff_lambda[l], diff_subln[l].reshape(1, DIFF_V),
                         tq, lambda_init)
        merged = _merge(h, o_a.reshape(t, -1), o_b.reshape(t, -1), wg, w_mla_o[l].astype(BF16),
                        w_diff_o[l].astype(BF16), tm, tiles.merge_cols)
        x1, h2 = _outproj(merged, w_att_out[l].astype(BF16), xf, mod3, norm2_w[l].reshape(1, d), seq, tm)

        route_j, route_i = _peer_route(h2, w_peer_q[l].T.astype(BF16),
                                       peer_keys[l].reshape(2 * PEER_HEADS, PEER_NKEYS, PEER_HALF).astype(BF16),
                                       tiles.route_tokens)
        xf = _peer_dense(h2, peer_u[l].astype(BF16), peer_v[l].astype(BF16), route_j, route_i, x1, mod3,
                         seq, tm, tiles.expert_block)
    return xf.reshape(bsz, seq, d)
```
